```python
import math
import jax, jax.numpy as jnp
from jax import lax
import numpy as np

D_MODEL = 4096
BATCH = 8
SEQ = 2048
DEPTH = 4

CHUNK = 64
Q_BLOCK = 128
EPS = 1e-6
MLA_HEADS = 16
MLA_NOPE = 128
MLA_ROPE = 64
MLA_V = 128
Q_LORA = 1024
KV_LORA = 512
ROPE_THETA = 10000.0
HGRN_HEADS = 8
HGRN_DK = 128
HGRN_DV = 128
SB_HEADS = 8
SB_DH = 128
D_FF = 2 * D_MODEL
CONV_W = 3

MLA_W = MLA_HEADS * MLA_V
HGRN_KW = HGRN_HEADS * HGRN_DK
HGRN_W = HGRN_HEADS * HGRN_DV
SB_W = SB_HEADS * SB_DH
IN_SPLITS = (Q_LORA, KV_LORA, MLA_ROPE, HGRN_KW, HGRN_KW, HGRN_W, HGRN_W,
             SB_W, SB_W, SB_W, D_MODEL, D_MODEL, D_MODEL)
N_IN = sum(IN_SPLITS)

kernel_name = "hybrid_mla_hgrn2_stickbreak_convffn"


def rms_norm(x, w):
    xf = x.astype(jnp.float32)
    y = xf * lax.rsqrt(jnp.mean(xf * xf, axis=-1, keepdims=True) + EPS)
    return (y * w.astype(jnp.float32)).astype(x.dtype)


def rotary_tables(seq):
    inv = 1.0 / (ROPE_THETA ** (jnp.arange(0, MLA_ROPE, 2, dtype=jnp.float32) / MLA_ROPE))
    ang = jnp.arange(seq, dtype=jnp.float32)[:, None] * inv[None, :]
    return jnp.cos(ang), jnp.sin(ang)


def apply_rope(x, cos, sin):
    x1, x2 = jnp.split(x.astype(jnp.float32), 2, axis=-1)
    return jnp.concatenate([x1 * cos - x2 * sin, x2 * cos + x1 * sin], axis=-1).astype(x.dtype)


def mla_attention(c_q, c_kv, k_rope, q_norm, kv_norm, w_uq, w_uk, w_uv, cos, sin):
    b, s, _ = c_q.shape
    q = (rms_norm(c_q, q_norm) @ w_uq).reshape(b, s, MLA_HEADS, MLA_NOPE + MLA_ROPE)
    q_nope = q[..., :MLA_NOPE]
    q_pe = apply_rope(q[..., MLA_NOPE:], cos[:, None, :], sin[:, None, :])
    ckv = rms_norm(c_kv, kv_norm)
    k_nope = (ckv @ w_uk).reshape(b, s, MLA_HEADS, MLA_NOPE)
    v = (ckv @ w_uv).reshape(b, s, MLA_HEADS, MLA_V)
    k_pe = apply_rope(k_rope, cos, sin)
    scale = (MLA_NOPE + MLA_ROPE) ** -0.5
    outs = []
    for start in range(0, s, Q_BLOCK):
        end = start + Q_BLOCK
        sc = (jnp.einsum('bqhd,bkhd->bhqk', q_nope[:, start:end], k_nope[:, :end])
              + jnp.einsum('bqhr,bkr->bhqk', q_pe[:, start:end], k_pe[:, :end]))
        sc = sc.astype(jnp.float32) * scale
        mask = (np.arange(end) // CHUNK)[None, :] <= (np.arange(start, end) // CHUNK)[:, None]
        p = jax.nn.softmax(jnp.where(mask, sc, -jnp.inf), axis=-1).astype(v.dtype)
        outs.append(jnp.einsum('bhqk,bkhd->bqhd', p, v[:, :end]))
    return jnp.concatenate(outs, axis=1).reshape(b, s, MLA_W)


def hgrn2_recurrence(q, k, v, log_f):
    b, s, h, dk = q.shape
    dv = v.shape[-1]
    n = s // CHUNK

    def to_chunks(t):
        return t.reshape(b, n, CHUNK, h, t.shape[-1]).transpose(1, 0, 3, 2, 4)

    causal = jnp.asarray(np.tril(np.ones((CHUNK, CHUNK), dtype=bool)))[:, :, None]

    def step(state, inp):
        qc, kc, vc, gc = inp
        cum = jnp.cumsum(gc, axis=2)
        last = cum[:, :, -1:, :]
        diff = cum[:, :, :, None, :] - cum[:, :, None, :, :]
        decay = jnp.exp(jnp.where(causal, diff, -jnp.inf))
        scores = jnp.einsum('bhtd,bhsd,bhtsd->bhts', qc, kc, decay)
        o = (jnp.einsum('bhts,bhsv->bhtv', scores, vc)
             + jnp.einsum('bhtd,bhdv->bhtv', qc * jnp.exp(cum), state))
        state = (jnp.exp(last[:, :, 0, :])[..., None] * state
                 + jnp.einsum('bhsd,bhsv->bhdv', kc * jnp.exp(last - cum), vc))
        return state, o

    s0 = jnp.zeros((b, h, dk, dv), jnp.float32)
    _, o = lax.scan(step, s0, (to_chunks(q), to_chunks(k), to_chunks(v), to_chunks(log_f)))
    return o.transpose(1, 0, 3, 2, 4).reshape(b, s, h, dv)


def hgrn2_branch(hq, hf, hi, hg, lb, out_norm):
    b, s, _ = hq.shape
    kshape = (b, s, HGRN_HEADS, HGRN_DK)
    lbf = lb.astype(jnp.float32)
    log_f = jnp.logaddexp(jnp.log(lbf), jnp.log1p(-lbf) + jax.nn.log_sigmoid(hf.astype(jnp.float32)))
    k = -jnp.expm1(log_f)
    o = hgrn2_recurrence(hq.astype(jnp.float32).reshape(kshape), k.reshape(kshape),
                         hi.astype(jnp.float32).reshape(b, s, HGRN_HEADS, HGRN_DV),
                         log_f.reshape(kshape))
    o = o * lax.rsqrt(jnp.mean(o * o, axis=-1, keepdims=True) + EPS)
    o = o.reshape(b, s, HGRN_W) * out_norm.astype(jnp.float32) * jax.nn.silu(hg.astype(jnp.float32))
    return o.astype(hq.dtype)


def stick_breaking_attention(q, k, v):
    b, s = q.shape[:2]
    scale = SB_DH ** -0.5
    outs = []
    for start in range(0, s, Q_BLOCK):
        end = start + Q_BLOCK
        z = jnp.einsum('bqhd,bkhd->bhqk', q[:, start:end], k[:, :end]).astype(jnp.float32) * scale
        strict = np.arange(end)[None, :] < np.arange(start, end)[:, None]
        log_1m = jnp.where(strict, jax.nn.log_sigmoid(-z), 0.0)
        tail = lax.cumsum(log_1m, axis=3, reverse=True) - log_1m
        a = jnp.where(strict, jnp.exp(jax.nn.log_sigmoid(z) + tail), 0.0)
        outs.append(jnp.einsum('bhqk,bkhd->bqhd', a.astype(v.dtype), v[:, :end]))
    return jnp.concatenate(outs, axis=1).reshape(b, s, SB_W)


def conv_ffn(h, w_up, conv_w, w_down):
    u = h @ w_up
    c = u.shape[-1]
    u = lax.conv_general_dilated(u, conv_w[:, None, :].astype(u.dtype), window_strides=(1,),
                                 padding=((CONV_W - 1, 0),),
                                 dimension_numbers=('NWC', 'WIO', 'NWC'),
                                 feature_group_count=c)
    gate, up = jnp.split(u, 2, axis=-1)
    return (jax.nn.silu(gate) * up) @ w_down


def setup_inputs(seed: int = 0) -> dict:
    key = jax.random.key(seed)
    ks = jax.random.split(key, 20)
    L, D = DEPTH, D_MODEL
    f32 = jnp.float32

    def nrm(k, shape, fan_in):
        return jax.random.normal(k, shape, f32) * (fan_in ** -0.5)

    def gain(k, shape):
        return 1.0 + 0.02 * jax.random.normal(k, shape, f32)

    return {
        "x": jax.random.normal(ks[0], (BATCH, SEQ, D), f32),
        "w_in": nrm(ks[1], (L, D, N_IN), D),
        "mla_q_norm": gain(ks[2], (L, Q_LORA)),
        "mla_kv_norm": gain(ks[3], (L, KV_LORA)),
        "mla_w_uq": nrm(ks[4], (L, Q_LORA, MLA_HEADS * (MLA_NOPE + MLA_ROPE)), Q_LORA),
        "mla_w_uk": nrm(ks[5], (L, KV_LORA, MLA_HEADS * MLA_NOPE), KV_LORA),
        "mla_w_uv": nrm(ks[6], (L, KV_LORA, MLA_HEADS * MLA_V), KV_LORA),
        "hgrn_lb_logits": 0.5 * jax.random.normal(ks[7], (L, HGRN_KW), f32),
        "hgrn_out_norm": gain(ks[8], (L, HGRN_W)),
        "w_branch_mla": nrm(ks[9], (L, MLA_W, D), MLA_W),
        "w_branch_hgrn": nrm(ks[10], (L, HGRN_W, D), HGRN_W),
        "w_branch_sb": nrm(ks[11], (L, SB_W, D), SB_W),
        "w_out": nrm(ks[12], (L, D, D), D),
        "mix_norm": gain(ks[13], (L, D)),
        "ffn_norm": gain(ks[14], (L, D)),
        "ffn_w_up": nrm(ks[15], (L, D, 2 * D_FF), D),
        "ffn_conv": nrm(ks[16], (L, CONV_W, 2 * D_FF), CONV_W),
        "ffn_w_down": nrm(ks[17], (L, D_FF, D), D_FF),
        "final_norm": gain(ks[18], (D,)),
    }


def reference(x, w_in, mla_q_norm, mla_kv_norm, mla_w_uq, mla_w_uk, mla_w_uv,
              hgrn_lb_logits, hgrn_out_norm, w_branch_mla, w_branch_hgrn, w_branch_sb,
              w_out, mix_norm, ffn_norm, ffn_w_up, ffn_conv, ffn_w_down, final_norm):
    b, s, _ = x.shape
    cos, sin = rotary_tables(s)
    lb_all = jnp.cumsum(jax.nn.softmax(hgrn_lb_logits.astype(jnp.float32), axis=0), axis=0)
    lb_all = lb_all - lb_all[0:1]
    split_idx = np.cumsum(IN_SPLITS)[:-1].tolist()
    for l in range(DEPTH):
        h = rms_norm(x, mix_norm[l])
        (c_q, c_kv, k_rope, hq, hf, hi, hg, sq, sk, sv, ga, gb, gc) = jnp.split(h @ w_in[l], split_idx, axis=-1)
        y_a = mla_attention(c_q, c_kv, k_rope, mla_q_norm[l], mla_kv_norm[l],
                            mla_w_uq[l], mla_w_uk[l], mla_w_uv[l], cos, sin)
        y_b = hgrn2_branch(hq, hf, hi, hg, lb_all[l], hgrn_out_norm[l])
        sb_shape = (b, s, SB_HEADS, SB_DH)
        y_c = stick_breaking_attention(sq.reshape(sb_shape), sk.reshape(sb_shape), sv.reshape(sb_shape))
        merged = (jax.nn.sigmoid(ga) * (y_a @ w_branch_mla[l])
                  + jax.nn.sigmoid(gb) * (y_b @ w_branch_hgrn[l])
                  + jax.nn.sigmoid(gc) * (y_c @ w_branch_sb[l]))
        x = x + merged @ w_out[l]
        x = x + conv_ffn(rms_norm(x, ffn_norm[l]), ffn_w_up[l], ffn_conv[l], ffn_w_down[l])
    return rms_norm(x, final_norm)
```

```python
import functools

import numpy as np
import jax
import jax.numpy as jnp
from jax import lax
from jax.experimental import pallas as pl
from jax.experimental.pallas import tpu as pltpu

CHUNK = 64
EPS = 1e-6
MLA_HEADS = 16
MLA_NOPE = 128
MLA_ROPE = 64
MLA_V = 128
ROPE_THETA = 10000.0
HGRN_HEADS = 8
HGRN_DK = 128
HGRN_DV = 128
SB_HEADS = 8
SB_DH = 128
CONV_W = 3

LANES = 128
SUBLANES = 8
VMEM_LIMIT_BYTES = 56 * 1024 * 1024
ROW_TILE = 1024
COL_TILE = 1024
ATTN_TILE = 256
NEG_BIG = -1e30
assert CHUNK & (CHUNK - 1) == 0

F32 = jnp.float32
BF16 = jnp.bfloat16
_NT = (((1,), (1,)), ((), ()))


def _tile(n, pref, align):
    t = (min(pref, n) // align) * align
    while t >= align:
        if n % t == 0:
            return t
        t -= align
    return n


def _cparams(ndims):
    return pltpu.CompilerParams(dimension_semantics=("arbitrary",) * ndims,
                                vmem_limit_bytes=VMEM_LIMIT_BYTES)


def _rms_rows(x, w):
    ms = jnp.mean(x * x, axis=-1, keepdims=True)
    return x * lax.rsqrt(ms + EPS) * w


def _rmsnorm_body(x_ref, w_ref, o_ref):
    o_ref[...] = _rms_rows(x_ref[...], w_ref[...]).astype(o_ref.dtype)


def _rmsnorm(x2d, w, out_dtype):
    t, d = x2d.shape
    tm = _tile(t, 512, SUBLANES)
    return pl.pallas_call(
        _rmsnorm_body,
        grid=(t // tm,),
        in_specs=[pl.BlockSpec((tm, d), lambda i: (i, 0)),
                  pl.BlockSpec((1, d), lambda i: (0, 0))],
        out_specs=pl.BlockSpec((tm, d), lambda i: (i, 0)),
        out_shape=jax.ShapeDtypeStruct((t, d), out_dtype),
        compiler_params=_cparams(1),
        name="rmsnorm",
    )(x2d, w.reshape(1, d))


def _mm_body(*refs, norm, act, scale):
    if norm:
        a_ref, nw_ref, b_ref, o_ref = refs
        a = _rms_rows(a_ref[...], nw_ref[...]).astype(BF16)
    else:
        a_ref, b_ref, o_ref = refs
        a = a_ref[...]
    acc = jnp.dot(a, b_ref[...], preferred_element_type=F32)
    if act == "sigmoid":
        acc = jax.nn.sigmoid(acc)
    if scale is not None:
        acc = acc * scale
    o_ref[...] = acc.astype(o_ref.dtype)


def _mm(a, b, *, k, a_blk=0, b_col0=0, n, out_dtype, norm_w=None, act=None, scale=None, name="mm"):
    m = a.shape[0]
    tm = _tile(m, ROW_TILE, SUBLANES)
    tn = _tile(int(np.gcd(n, b_col0)) if b_col0 else n, COL_TILE, LANES)
    assert n % tn == 0 and b_col0 % tn == 0 and b.shape[0] == k
    j0 = b_col0 // tn
    in_specs = [pl.BlockSpec((tm, k), lambda i, j: (i, a_blk))]
    args = [a]
    if norm_w is not None:
        in_specs.append(pl.BlockSpec((1, k), lambda i, j: (0, 0)))
        args.append(norm_w.reshape(1, k))
    in_specs.append(pl.BlockSpec((k, tn), lambda i, j: (0, j0 + j)))
    args.append(b)
    return pl.pallas_call(
        functools.partial(_mm_body, norm=norm_w is not None, act=act, scale=scale),
        grid=(m // tm, n // tn),
        in_specs=in_specs,
        out_specs=pl.BlockSpec((tm, tn), lambda i, j: (i, j)),
        out_shape=jax.ShapeDtypeStruct((m, n), out_dtype),
        compiler_params=_cparams(2),
        name=name,
    )(*args)


def _qpe_body(a_ref, nw_ref, w_ref, wr_ref, cos_ref, sin_ref, o_ref, *, scale, reps):
    a = _rms_rows(a_ref[...], nw_ref[...]).astype(BF16)
    p = jnp.dot(a, w_ref[...], preferred_element_type=F32)
    pr = jnp.dot(a, wr_ref[...], preferred_element_type=F32)
    cos = jnp.tile(cos_ref[...], (1, reps))
    sin = jnp.tile(sin_ref[...], (1, reps))
    o_ref[...] = ((p * cos + pr * sin) * scale).astype(o_ref.dtype)


def _qpe(mla_in, norm_w, w_pe, w_per, cos2, sin2, *, k, seq, scale):
    m = mla_in.shape[0]
    n = w_pe.shape[1]
    tm = _tile(seq, ROW_TILE, SUBLANES)
    tn = _tile(n, COL_TILE // 2, LANES)
    per_seq = seq // tm
    return pl.pallas_call(
        functools.partial(_qpe_body, scale=scale, reps=tn // LANES),
        grid=(m // tm, n // tn),
        in_specs=[pl.BlockSpec((tm, k), lambda i, j: (i, 0)),
                  pl.BlockSpec((1, k), lambda i, j: (0, 0)),
                  pl.BlockSpec((k, tn), lambda i, j: (0, j)),
                  pl.BlockSpec((k, tn), lambda i, j: (0, j)),
                  pl.BlockSpec((tm, LANES), lambda i, j: (i % per_seq, 0)),
                  pl.BlockSpec((tm, LANES), lambda i, j: (i % per_seq, 0))],
        out_specs=pl.BlockSpec((tm, tn), lambda i, j: (i, j)),
        out_shape=jax.ShapeDtypeStruct((m, n), BF16),
        compiler_params=_cparams(2),
        name="mla_q_rope",
    )(mla_in, norm_w.reshape(1, k), w_pe, w_per, cos2, sin2)


def _mla_body(qn_ref, qpe_ref, kn_ref, v_ref, kpe_ref, kper_ref, cos_ref, sin_ref,
              o_ref, kcat_ref, *, tq):
    h = pl.program_id(1)
    i = pl.program_id(2)

    @pl.when(i == 0)
    def _():
        kcat_ref[:, :LANES] = kn_ref[...]
        kcat_ref[:, LANES:] = (kpe_ref[...] * cos_ref[...]
                               + kper_ref[...] * sin_ref[...]).astype(BF16)

    lane = lax.broadcasted_iota(jnp.int32, (tq, LANES), 1)
    keep = (lane >= MLA_ROPE).astype(jnp.int32) == h % 2
    qpe = jnp.where(keep, qpe_ref[...], jnp.zeros((tq, LANES), BF16))
    q = jnp.concatenate([qn_ref[...], qpe], axis=1)

    row = lax.broadcasted_iota(jnp.int32, (tq, tq), 0)
    col = lax.broadcasted_iota(jnp.int32, (tq, tq), 1)
    visible = col <= jnp.bitwise_or(row, CHUNK - 1)

    def block(j, carry, masked):
        m_prev, l_prev, acc = carry
        r0 = pl.multiple_of(j * tq, tq)
        k = kcat_ref[pl.ds(r0, tq), :]
        v = v_ref[pl.ds(r0, tq), :]
        s = lax.dot_general(q, k, _NT, preferred_element_type=F32)
        if masked:
            s = jnp.where(visible, s, NEG_BIG)
        m_new = jnp.maximum(m_prev, jnp.max(s, axis=1, keepdims=True))
        alpha = jnp.exp(m_prev - m_new)
        p = jnp.exp(s - m_new)
        l_new = alpha * l_prev + jnp.sum(p, axis=1, keepdims=True)
        acc = alpha * acc + jnp.dot(p.astype(BF16), v, preferred_element_type=F32)
        return m_new, l_new, acc

    init = (jnp.full((tq, 1), NEG_BIG, F32), jnp.zeros((tq, 1), F32),
            jnp.zeros((tq, MLA_V), F32))
    carry = lax.fori_loop(0, i, lambda j, c: block(j, c, False), init)
    _, l_fin, acc = block(i, carry, True)
    o_ref[...] = (acc / l_fin).astype(o_ref.dtype)


def _mla_attention(qn, qpe, kv, mla_in, cos2, sin2, *, batch, seq):
    t = qn.shape[0]
    tq = _tile(seq, ATTN_TILE, CHUNK)
    nq = seq // tq
    kpe_blk = mla_in.shape[1] // LANES - 2
    return pl.pallas_call(
        functools.partial(_mla_body, tq=tq),
        grid=(batch, MLA_HEADS, nq),
        in_specs=[pl.BlockSpec((tq, LANES), lambda b, h, i: (b * nq + i, h)),
                  pl.BlockSpec((tq, LANES), lambda b, h, i: (b * nq + i, h // 2)),
                  pl.BlockSpec((seq, LANES), lambda b, h, i: (b, h)),
                  pl.BlockSpec((seq, LANES), lambda b, h, i: (b, MLA_HEADS + h)),
                  pl.BlockSpec((seq, LANES), lambda b, h, i: (b, kpe_blk)),
                  pl.BlockSpec((seq, LANES), lambda b, h, i: (b, kpe_blk + 1)),
                  pl.BlockSpec((seq, LANES), lambda b, h, i: (0, 0)),
                  pl.BlockSpec((seq, LANES), lambda b, h, i: (0, 0))],
        out_specs=pl.BlockSpec((tq, LANES), lambda b, h, i: (b * nq + i, h)),
        out_shape=jax.ShapeDtypeStruct((t, MLA_HEADS * MLA_V), BF16),
        scratch_shapes=[pltpu.VMEM((seq, 2 * LANES), BF16)],
        compiler_params=_cparams(3),
        name="mla_attention",
    )(qn, qpe, kv, kv, mla_in, mla_in, cos2, sin2)


def _sb_body(q_ref, k_ref, v_ref, o_ref, *, tq, scale):
    i = pl.program_id(2)
    q = q_ref[...]
    row = lax.broadcasted_iota(jnp.int32, (tq, tq), 0)
    col = lax.broadcasted_iota(jnp.int32, (tq, tq), 1)
    strict = col < row
    later = (row > col).astype(BF16)

    def block(j, carry, masked):
        acc, tail_c = carry
        r0 = pl.multiple_of(j * tq, tq)
        k = k_ref[pl.ds(r0, tq), :]
        v = v_ref[pl.ds(r0, tq), :]
        z = lax.dot_general(q, k, _NT, preferred_element_type=F32) * scale
        sp = jnp.maximum(z, 0.0) + jnp.log1p(jnp.exp(-jnp.abs(z)))
        log_1m = -sp
        if masked:
            log_1m = jnp.where(strict, log_1m, 0.0)
        hi = log_1m.astype(BF16)
        lo = (log_1m - hi.astype(F32)).astype(BF16)
        tail = (jnp.dot(hi, later, preferred_element_type=F32)
                + jnp.dot(lo, later, preferred_element_type=F32) + tail_c)
        a = jnp.exp(z - sp + tail)
        if masked:
            a = jnp.where(strict, a, 0.0)
        acc = acc + jnp.dot(a.astype(BF16), v, preferred_element_type=F32)
        tail_c = tail_c + jnp.sum(log_1m, axis=1, keepdims=True)
        return acc, tail_c

    carry = block(i, (jnp.zeros((tq, SB_DH), F32), jnp.zeros((tq, 1), F32)), True)
    acc, _ = lax.fori_loop(0, i, lambda jj, c: block(i - 1 - jj, c, False), carry)
    o_ref[...] = acc.astype(o_ref.dtype)


def _sb_attention(sb, *, batch, seq):
    t = sb.shape[0]
    tq = _tile(seq, ATTN_TILE, SUBLANES)
    nq = seq // tq
    return pl.pallas_call(
        functools.partial(_sb_body, tq=tq, scale=SB_DH ** -0.5),
        grid=(batch, SB_HEADS, nq),
        in_specs=[pl.BlockSpec((tq, LANES), lambda b, h, i: (b * nq + i, h)),
                  pl.BlockSpec((seq, LANES), lambda b, h, i: (b, SB_HEADS + h)),
                  pl.BlockSpec((seq, LANES), lambda b, h, i: (b, 2 * SB_HEADS + h))],
        out_specs=pl.BlockSpec((tq, LANES), lambda b, h, i: (b * nq + i, h)),
        out_shape=jax.ShapeDtypeStruct((t, SB_HEADS * SB_DH), BF16),
        compiler_params=_cparams(3),
        name="sb_attention",
    )(sb, sb, sb)


def _hgrn_body(q_ref, f_ref, v_ref, g_ref, loglb_ref, log1mlb_ref, omlb_ref, onorm_ref,
               o_ref, cum_ref, k_ref, state_ref, *, seq):
    c = CHUNK
    row = lax.broadcasted_iota(jnp.int32, (c, c), 0)
    col = lax.broadcasted_iota(jnp.int32, (c, c), 1)
    tril = (col <= row).astype(F32)
    sub = lax.broadcasted_iota(jnp.int32, (SUBLANES, HGRN_DK), 0)
    log_lb = loglb_ref[...]
    log_1mlb = log1mlb_ref[...]
    om_lb = omlb_ref[...]
    onorm = onorm_ref[...]
    state_ref[...] = jnp.zeros_like(state_ref)

    def chunk(ci, _):
        r0 = pl.multiple_of(ci * c, c)
        z = f_ref[pl.ds(r0, c), :]
        log_sig = jnp.minimum(z, 0.0) - jnp.log1p(jnp.exp(-jnp.abs(z)))
        y = log_1mlb + log_sig
        log_f = jnp.maximum(log_lb, y) + jnp.log1p(jnp.exp(-jnp.abs(log_lb - y)))
        kk = om_lb * jnp.exp(log_sig - z)
        cum = jnp.dot(tril, log_f, preferred_element_type=F32, precision=lax.Precision.HIGHEST)
        cum_ref[...] = cum
        k_ref[...] = kk
        qc = q_ref[pl.ds(r0, c), :]
        last = cum[c - 1:c, :]
        state_t = state_ref[...]
        qe = (qc * jnp.exp(cum)).astype(BF16)
        o_inter = lax.dot_general(qe, state_t.astype(BF16), _NT, preferred_element_type=F32)
        groups = c // SUBLANES
        o_parts = [o_inter[g * SUBLANES:(g + 1) * SUBLANES] for g in range(groups)]
        for j in range(c):
            kj = k_ref[pl.ds(j, 1), :]
            cj = cum_ref[pl.ds(j, 1), :]
            vj = v_ref[pl.ds(r0 + j, 1), :]
            g0 = j // SUBLANES
            for g in range(g0, groups):
                qv = qc[g * SUBLANES:(g + 1) * SUBLANES]
                cv = cum[g * SUBLANES:(g + 1) * SUBLANES]
                w = qv * kj * jnp.exp(cv - cj)
                if g == g0 and j % SUBLANES:
                    w = jnp.where(sub >= j % SUBLANES, w, 0.0)
                o_parts[g] = o_parts[g] + jnp.sum(w, axis=1, keepdims=True) * vj
        o = jnp.concatenate(o_parts, axis=0)
        kd = (kk * jnp.exp(last - cum)).astype(BF16)
        vt = v_ref[pl.ds(r0, c), :].T.astype(BF16)
        state_ref[...] = state_t * jnp.exp(last) + jnp.dot(vt, kd, preferred_element_type=F32)
        o = o * lax.rsqrt(jnp.mean(o * o, axis=-1, keepdims=True) + EPS)
        hg = g_ref[pl.ds(r0, c), :]
        o = o * onorm * (hg * jax.nn.sigmoid(hg))
        o_ref[pl.ds(r0, c), :] = o.astype(o_ref.dtype)
        return 0

    lax.fori_loop(0, seq // c, chunk, 0)


def _hgrn(hg_in, log_lb, log_1mlb, om_lb, out_norm, *, batch, seq):
    t = hg_in.shape[0]
    nh = HGRN_HEADS
    vec = lambda off: pl.BlockSpec((1, LANES), lambda b, h: (0, off + h))
    return pl.pallas_call(
        functools.partial(_hgrn_body, seq=seq),
        grid=(batch, nh),
        in_specs=[pl.BlockSpec((seq, LANES), lambda b, h: (b, h)),
                  pl.BlockSpec((seq, LANES), lambda b, h: (b, nh + h)),
                  pl.BlockSpec((seq, LANES), lambda b, h: (b, 2 * nh + h)),
                  pl.BlockSpec((seq, LANES), lambda b, h: (b, 3 * nh + h)),
                  vec(0), vec(0), vec(0), vec(0)],
        out_specs=pl.BlockSpec((seq, LANES), lambda b, h: (b, h)),
        out_shape=jax.ShapeDtypeStruct((t, nh * HGRN_DV), BF16),
        scratch_shapes=[pltpu.VMEM((CHUNK, HGRN_DK), F32),
                        pltpu.VMEM((CHUNK, HGRN_DK), F32),
                        pltpu.VMEM((HGRN_DV, HGRN_DK), F32)],
        compiler_params=_cparams(2),
        name="hgrn2",
    )(hg_in, hg_in, hg_in, hg_in, log_lb.reshape(1, -1), log_1mlb.reshape(1, -1),
      om_lb.reshape(1, -1), out_norm.reshape(1, -1))


def _merge_body(ya_ref, yb_ref, yc_ref, wa_ref, wb_ref, wc_ref, ga_ref, gb_ref, gc_ref, o_ref):
    acc = ga_ref[...].astype(F32) * jnp.dot(ya_ref[...], wa_ref[...], preferred_element_type=F32)
    acc = acc + gb_ref[...].astype(F32) * jnp.dot(yb_ref[...], wb_ref[...], preferred_element_type=F32)
    acc = acc + gc_ref[...].astype(F32) * jnp.dot(yc_ref[...], wc_ref[...], preferred_element_type=F32)
    o_ref[...] = acc.astype(o_ref.dtype)


def _merge(ya, yb, yc, wa, wb, wc, gates):
    m = ya.shape[0]
    d = wa.shape[1]
    tm = _tile(m, ROW_TILE, SUBLANES)
    tn = _tile(d, COL_TILE // 2, LANES)
    nj = d // tn
    row_blk = lambda arr: pl.BlockSpec((tm, arr.shape[1]), lambda i, j: (i, 0))
    col_blk = lambda arr: pl.BlockSpec((arr.shape[0], tn), lambda i, j: (0, j))
    gate_blk = lambda g: pl.BlockSpec((tm, tn), lambda i, j: (i, g * nj + j))
    return pl.pallas_call(
        _merge_body,
        grid=(m // tm, nj),
        in_specs=[row_blk(ya), row_blk(yb), row_blk(yc), col_blk(wa), col_blk(wb), col_blk(wc),
                  gate_blk(0), gate_blk(1), gate_blk(2)],
        out_specs=pl.BlockSpec((tm, tn), lambda i, j: (i, j)),
        out_shape=jax.ShapeDtypeStruct((m, d), BF16),
        compiler_params=_cparams(2),
        name="branch_merge",
    )(ya, yb, yc, wa, wb, wc, gates, gates, gates)


def _mm_res_body(a_ref, w_ref, res_ref, o_ref):
    kk = pl.program_id(2)
    part = jnp.dot(a_ref[...], w_ref[...], preferred_element_type=F32)

    @pl.when(kk == 0)
    def _():
        o_ref[...] = res_ref[...] + part

    @pl.when(kk != 0)
    def _():
        o_ref[...] += part


def _mm_res(a, w, res, *, name):
    m, k = a.shape
    n = w.shape[1]
    tm = _tile(m, ROW_TILE, SUBLANES)
    tn = _tile(n, COL_TILE // 2, LANES)
    tk = _tile(k, 4 * COL_TILE, LANES)
    return pl.pallas_call(
        _mm_res_body,
        grid=(m // tm, n // tn, k // tk),
        in_specs=[pl.BlockSpec((tm, tk), lambda i, j, kk: (i, kk)),
                  pl.BlockSpec((tk, tn), lambda i, j, kk: (kk, j)),
                  pl.BlockSpec((tm, tn), lambda i, j, kk: (i, j))],
        out_specs=pl.BlockSpec((tm, tn), lambda i, j, kk: (i, j)),
        out_shape=jax.ShapeDtypeStruct((m, n), F32),
        compiler_params=_cparams(3),
        name=name,
    )(a, w, res)


def _ffn_up_body(a_ref, wg_ref, wu_ref, cg_ref, cu_ref, o_ref, stage_ref, *, tm, tn, per_seq):
    i = pl.program_id(1)
    pad = SUBLANES

    @pl.when(i % per_seq == 0)
    def _():
        stage_ref[0:pad, :] = jnp.zeros((pad, 2 * tn), F32)

    @pl.when(i % per_seq != 0)
    def _():
        stage_ref[0:pad, :] = stage_ref[tm:tm + pad, :]

    a = a_ref[...]
    stage_ref[pad:pad + tm, 0:tn] = jnp.dot(a, wg_ref[...], preferred_element_type=F32)
    stage_ref[pad:pad + tm, tn:2 * tn] = jnp.dot(a, wu_ref[...], preferred_element_type=F32)

    def conv(lo, hi, c_ref):
        cw = c_ref[...]
        out = cw[CONV_W - 1:CONV_W, :] * stage_ref[pad:pad + tm, lo:hi]
        for back in range(1, CONV_W):
            out = out + (cw[CONV_W - 1 - back:CONV_W - back, :]
                         * stage_ref[pad - back:pad - back + tm, lo:hi])
        return out

    gate = conv(0, tn, cg_ref)
    up = conv(tn, 2 * tn, cu_ref)
    o_ref[...] = (gate * jax.nn.sigmoid(gate) * up).astype(o_ref.dtype)


def _ffn_up(a, w_up, conv_w, *, seq):
    m, k = a.shape
    d_ff = w_up.shape[1] // 2
    tm = _tile(seq, ROW_TILE, SUBLANES)
    tn = _tile(d_ff, COL_TILE // 2, LANES)
    nj = d_ff // tn
    return pl.pallas_call(
        functools.partial(_ffn_up_body, tm=tm, tn=tn, per_seq=seq // tm),
        grid=(nj, m // tm),
        in_specs=[pl.BlockSpec((tm, k), lambda j, i: (i, 0)),
                  pl.BlockSpec((k, tn), lambda j, i: (0, j)),
                  pl.BlockSpec((k, tn), lambda j, i: (0, nj + j)),
                  pl.BlockSpec((CONV_W, tn), lambda j, i: (0, j)),
                  pl.BlockSpec((CONV_W, tn), lambda j, i: (0, nj + j))],
        out_specs=pl.BlockSpec((tm, tn), lambda j, i: (i, j)),
        out_shape=jax.ShapeDtypeStruct((m, d_ff), BF16),
        scratch_shapes=[pltpu.VMEM((tm + SUBLANES, 2 * tn), F32)],
        compiler_params=_cparams(2),
        name="ffn_up_conv_gate",
    )(a, w_up, w_up, conv_w, conv_w)


def _rot_half_cols(w):
    shape = w.shape
    w = w.reshape(shape[0], -1, 2, MLA_ROPE // 2)
    return jnp.stack([-w[:, :, 1], w[:, :, 0]], axis=2).reshape(shape)


def _split_cols(w, sizes):
    out, c = [], 0
    for s in sizes:
        out.append(w[:, c:c + s])
        c += s
    return out


def kernel(x, w_in, mla_q_norm, mla_kv_norm, mla_w_uq, mla_w_uk, mla_w_uv, hgrn_lb_logits,
           hgrn_out_norm, w_branch_mla, w_branch_hgrn, w_branch_sb, w_out, mix_norm, ffn_norm,
           ffn_w_up, ffn_conv, ffn_w_down, final_norm):
    batch, seq, d = x.shape
    depth = w_in.shape[0]
    t = batch * seq
    q_lora = mla_q_norm.shape[1]
    kv_lora = mla_kv_norm.shape[1]
    hk = HGRN_HEADS * HGRN_DK
    hv = HGRN_HEADS * HGRN_DV
    sbw = SB_HEADS * SB_DH
    qk_dim = MLA_NOPE + MLA_ROPE
    splits = (q_lora, kv_lora, MLA_ROPE, hk, hk, hv, hv, sbw, sbw, sbw, d, d, d)

    inv = 1.0 / (ROPE_THETA ** (jnp.arange(0, MLA_ROPE, 2, dtype=F32) / MLA_ROPE))
    ang = jnp.arange(seq, dtype=F32)[:, None] * inv[None, :]
    cos2 = jnp.tile(jnp.cos(ang), (1, 2 * LANES // MLA_ROPE))
    sin2 = jnp.tile(jnp.sin(ang), (1, 2 * LANES // MLA_ROPE))

    lb_all = jnp.cumsum(jax.nn.softmax(hgrn_lb_logits.astype(F32), axis=0), axis=0)
    lb_all = lb_all - lb_all[0:1]

    xf = x.reshape(t, d)
    for l in range(depth):
        (w_cq, w_ckv, w_kr, w_hq, w_hf, w_hi, w_hg, w_sq, w_sk, w_sv, w_ga, w_gb, w_gc) = _split_cols(w_in[l], splits)
        w_kr_rot = _rot_half_cols(w_kr)
        w_in_x = jnp.concatenate(
            [w_ga, w_gb, w_gc, w_hq, w_hf, w_hi, w_hg, w_sq, w_sk, w_sv,
             w_cq, w_ckv, w_kr, w_kr, w_kr_rot, w_kr_rot], axis=1).astype(BF16)
        n_gate, n_hgrn, n_sb = 3 * d, 2 * hk + 2 * hv, 3 * sbw
        n_mla = q_lora + kv_lora + 2 * LANES
        uq = mla_w_uq[l].reshape(q_lora, MLA_HEADS, qk_dim)
        w_qn = uq[:, :, :MLA_NOPE].reshape(q_lora, MLA_HEADS * MLA_NOPE).astype(BF16)
        w_qpe_f = uq[:, :, MLA_NOPE:].reshape(q_lora, MLA_HEADS * MLA_ROPE)
        w_qpe = w_qpe_f.astype(BF16)
        w_qper = _rot_half_cols(w_qpe_f).astype(BF16)
        w_kv = jnp.concatenate([mla_w_uk[l], mla_w_uv[l]], axis=1).astype(BF16)
        lb = lb_all[l]

        hn = _rmsnorm(xf, mix_norm[l], BF16)
        gates = _mm(hn, w_in_x, k=d, b_col0=0, n=n_gate, out_dtype=BF16, act="sigmoid", name="in_proj_gates")
        hg_in = _mm(hn, w_in_x, k=d, b_col0=n_gate, n=n_hgrn, out_dtype=F32, name="in_proj_hgrn")
        sb_in = _mm(hn, w_in_x, k=d, b_col0=n_gate + n_hgrn, n=n_sb, out_dtype=BF16, name="in_proj_sb")
        mla_in = _mm(hn, w_in_x, k=d, b_col0=n_gate + n_hgrn + n_sb, n=n_mla, out_dtype=F32,
                     name="in_proj_mla")

        scale = qk_dim ** -0.5
        qn = _mm(mla_in, w_qn, k=q_lora, a_blk=0, n=w_qn.shape[1], out_dtype=BF16,
                 norm_w=mla_q_norm[l], scale=scale, name="mla_q_nope")
        qpe = _qpe(mla_in, mla_q_norm[l], w_qpe, w_qper, cos2, sin2, k=q_lora, seq=seq, scale=scale)
        kv = _mm(mla_in, w_kv, k=kv_lora, a_blk=q_lora // kv_lora, n=w_kv.shape[1], out_dtype=BF16,
                 norm_w=mla_kv_norm[l], name="mla_kv_up")
        y_a = _mla_attention(qn, qpe, kv, mla_in, cos2, sin2, batch=batch, seq=seq)

        y_b = _hgrn(hg_in, jnp.log(lb), jnp.log1p(-lb), 1.0 - lb, hgrn_out_norm[l], batch=batch, seq=seq)
        y_c = _sb_attention(sb_in, batch=batch, seq=seq)

        merged = _merge(y_a, y_b, y_c, w_branch_mla[l].astype(BF16), w_branch_hgrn[l].astype(BF16),
                        w_branch_sb[l].astype(BF16), gates)
        xf = _mm_res(merged, w_out[l].astype(BF16), xf, name="out_proj_residual")

        hn2 = _rmsnorm(xf, ffn_norm[l], BF16)
        act = _ffn_up(hn2, ffn_w_up[l].astype(BF16), ffn_conv[l], seq=seq)
        xf = _mm_res(act, ffn_w_down[l].astype(BF16), xf, name="ffn_down_residual")

    out = _rmsnorm(xf, final_norm, x.dtype)
    return out.reshape(batch, seq, d)
```

```python
import functools

import numpy as np
import jax
import jax.numpy as jnp
from jax import lax
from jax.experimental import pallas as pl
from jax.experimental.pallas import tpu as pltpu

CHUNK = 64
EPS = 1e-6
MLA_HEADS = 16
MLA_NOPE = 128
MLA_ROPE = 64
MLA_V = 128
ROPE_THETA = 10000.0
HGRN_HEADS = 8
HGRN_DK = 128
HGRN_DV = 128
SB_HEADS = 8
SB_DH = 128
CONV_W = 3
SUBCHUNK = 16
HGRN_UNROLL = 4

LANES = 128
SUBLANES = 8
VMEM_LIMIT_BYTES = 56 * 1024 * 1024
ROW_TILE = 1024
COL_TILE = 1024
MLA_TILE = 512
SB_Q_TILE = 512
SB_K_TILE = 256
NEG_BIG = -1e30
LOG2E = 1.4426950408889634
assert CHUNK & (CHUNK - 1) == 0

F32 = jnp.float32
BF16 = jnp.bfloat16
_NT = (((1,), (1,)), ((), ()))


def _tile(n, pref, align):
    t = (min(pref, n) // align) * align
    while t >= align:
        if n % t == 0:
            return t
        t -= align
    return n


def _cparams(ndims):
    return pltpu.CompilerParams(dimension_semantics=("arbitrary",) * ndims,
                                vmem_limit_bytes=VMEM_LIMIT_BYTES)


def _rms_rows(x, w):
    ms = jnp.mean(x * x, axis=-1, keepdims=True)
    return x * lax.rsqrt(ms + EPS) * w


def _rmsnorm_body(x_ref, w_ref, o_ref):
    o_ref[...] = _rms_rows(x_ref[...], w_ref[...]).astype(o_ref.dtype)


def _rmsnorm(x2d, w, out_dtype):
    t, d = x2d.shape
    tm = _tile(t, 512, SUBLANES)
    return pl.pallas_call(
        _rmsnorm_body,
        grid=(t // tm,),
        in_specs=[pl.BlockSpec((tm, d), lambda i: (i, 0)),
                  pl.BlockSpec((1, d), lambda i: (0, 0))],
        out_specs=pl.BlockSpec((tm, d), lambda i: (i, 0)),
        out_shape=jax.ShapeDtypeStruct((t, d), out_dtype),
        compiler_params=_cparams(1),
        name="rmsnorm",
    )(x2d, w.reshape(1, d))


def _mm_body(*refs, norm, act, scale):
    if norm:
        a_ref, nw_ref, b_ref, o_ref = refs
        a = _rms_rows(a_ref[...], nw_ref[...]).astype(BF16)
    else:
        a_ref, b_ref, o_ref = refs
        a = a_ref[...]
    acc = jnp.dot(a, b_ref[...], preferred_element_type=F32)
    if act == "sigmoid":
        acc = jax.nn.sigmoid(acc)
    if scale is not None:
        acc = acc * scale
    o_ref[...] = acc.astype(o_ref.dtype)


def _mm(a, b, *, k, a_blk=0, b_col0=0, n, out_dtype, norm_w=None, act=None, scale=None, name="mm"):
    m = a.shape[0]
    tm = _tile(m, ROW_TILE, SUBLANES)
    tn = _tile(int(np.gcd(n, b_col0)) if b_col0 else n, COL_TILE, LANES)
    assert n % tn == 0 and b_col0 % tn == 0 and b.shape[0] == k
    j0 = b_col0 // tn
    in_specs = [pl.BlockSpec((tm, k), lambda i, j: (i, a_blk))]
    args = [a]
    if norm_w is not None:
        in_specs.append(pl.BlockSpec((1, k), lambda i, j: (0, 0)))
        args.append(norm_w.reshape(1, k))
    in_specs.append(pl.BlockSpec((k, tn), lambda i, j: (0, j0 + j)))
    args.append(b)
    return pl.pallas_call(
        functools.partial(_mm_body, norm=norm_w is not None, act=act, scale=scale),
        grid=(m // tm, n // tn),
        in_specs=in_specs,
        out_specs=pl.BlockSpec((tm, tn), lambda i, j: (i, j)),
        out_shape=jax.ShapeDtypeStruct((m, n), out_dtype),
        compiler_params=_cparams(2),
        name=name,
    )(*args)


def _qpe_body(a_ref, nw_ref, w_ref, wr_ref, cos_ref, sin_ref, o_ref, *, scale, reps):
    a = _rms_rows(a_ref[...], nw_ref[...]).astype(BF16)
    p = jnp.dot(a, w_ref[...], preferred_element_type=F32)
    pr = jnp.dot(a, wr_ref[...], preferred_element_type=F32)
    cos = jnp.tile(cos_ref[...], (1, reps))
    sin = jnp.tile(sin_ref[...], (1, reps))
    o_ref[...] = ((p * cos + pr * sin) * scale).astype(o_ref.dtype)


def _qpe(mla_in, norm_w, w_pe, w_per, cos2, sin2, *, k, seq, scale):
    m = mla_in.shape[0]
    n = w_pe.shape[1]
    tm = _tile(seq, ROW_TILE, SUBLANES)
    tn = _tile(n, COL_TILE // 2, LANES)
    per_seq = seq // tm
    return pl.pallas_call(
        functools.partial(_qpe_body, scale=scale, reps=tn // LANES),
        grid=(m // tm, n // tn),
        in_specs=[pl.BlockSpec((tm, k), lambda i, j: (i, 0)),
                  pl.BlockSpec((1, k), lambda i, j: (0, 0)),
                  pl.BlockSpec((k, tn), lambda i, j: (0, j)),
                  pl.BlockSpec((k, tn), lambda i, j: (0, j)),
                  pl.BlockSpec((tm, LANES), lambda i, j: (i % per_seq, 0)),
                  pl.BlockSpec((tm, LANES), lambda i, j: (i % per_seq, 0))],
        out_specs=pl.BlockSpec((tm, tn), lambda i, j: (i, j)),
        out_shape=jax.ShapeDtypeStruct((m, n), BF16),
        compiler_params=_cparams(2),
        name="mla_q_rope",
    )(mla_in, norm_w.reshape(1, k), w_pe, w_per, cos2, sin2)


def _mla_body(qn_ref, qpe_ref, kn_ref, v_ref, kpe_ref, kper_ref, cos_ref, sin_ref,
              o_ref, kcat_ref, *, tq, tk):
    i = pl.program_id(2)

    @pl.when(i == 0)
    def _():
        kpe = (kpe_ref[...] * cos_ref[...] + kper_ref[...] * sin_ref[...]).astype(BF16)
        for hh in range(2):
            kcat_ref[hh, :, :LANES] = kn_ref[:, hh * LANES:(hh + 1) * LANES]
            kcat_ref[hh, :, LANES:] = kpe

    lane = lax.broadcasted_iota(jnp.int32, (tq, LANES), 1)
    qpe = qpe_ref[...]
    zero = jnp.zeros((tq, LANES), BF16)
    qs = [jnp.concatenate([qn_ref[:, 0:LANES], jnp.where(lane < MLA_ROPE, qpe, zero)], axis=1),
          jnp.concatenate([qn_ref[:, LANES:2 * LANES], jnp.where(lane >= MLA_ROPE, qpe, zero)], axis=1)]

    def block(r0, carry, visible):
        out = []
        for hh in range(2):
            m_prev, l_prev, acc = carry[hh]
            k = kcat_ref[hh, pl.ds(r0, tk), :]
            v = v_ref[pl.ds(r0, tk), hh * LANES:(hh + 1) * LANES]
            s = lax.dot_general(qs[hh], k, _NT, preferred_element_type=F32)
            if visible is not None:
                s = jnp.where(visible, s, NEG_BIG)
            m_new = jnp.maximum(m_prev, jnp.max(s, axis=1, keepdims=True))
            alpha = jnp.exp2(m_prev - m_new)
            p = jnp.exp2(s - m_new)
            l_new = alpha * l_prev + jnp.sum(p, axis=1, keepdims=True)
            acc = alpha * acc + jnp.dot(p.astype(BF16), v, preferred_element_type=F32)
            out.append((m_new, l_new, acc))
        return tuple(out)

    one = (jnp.full((tq, 1), NEG_BIG, F32), jnp.zeros((tq, 1), F32), jnp.zeros((tq, MLA_V), F32))
    carry = lax.fori_loop(0, i, lambda j, c: block(pl.multiple_of(j * tk, tk), c, None),
                          (one, one))
    row = lax.broadcasted_iota(jnp.int32, (tq, tk), 0)
    col = lax.broadcasted_iota(jnp.int32, (tq, tk), 1)
    visible = col <= jnp.bitwise_or(row, CHUNK - 1)
    carry = block(pl.multiple_of(i * tk, tk), carry, visible)
    for hh in range(2):
        _, l_fin, acc = carry[hh]
        o_ref[:, hh * LANES:(hh + 1) * LANES] = (acc / l_fin).astype(o_ref.dtype)


def _mla_attention(qn, qpe, kv, mla_in, cos2, sin2, *, batch, seq):
    t = qn.shape[0]
    tq = tk = _tile(seq, MLA_TILE, CHUNK)
    nq = seq // tq
    pairs = MLA_HEADS // 2
    kpe_blk = mla_in.shape[1] // LANES - 2
    return pl.pallas_call(
        functools.partial(_mla_body, tq=tq, tk=tk),
        grid=(batch, pairs, nq),
        in_specs=[pl.BlockSpec((tq, 2 * LANES), lambda b, h, i: (b * nq + i, h)),
                  pl.BlockSpec((tq, LANES), lambda b, h, i: (b * nq + i, h)),
                  pl.BlockSpec((seq, 2 * LANES), lambda b, h, i: (b, h)),
                  pl.BlockSpec((seq, 2 * LANES), lambda b, h, i: (b, pairs + h)),
                  pl.BlockSpec((seq, LANES), lambda b, h, i: (b, kpe_blk)),
                  pl.BlockSpec((seq, LANES), lambda b, h, i: (b, kpe_blk + 1)),
                  pl.BlockSpec((seq, LANES), lambda b, h, i: (0, 0)),
                  pl.BlockSpec((seq, LANES), lambda b, h, i: (0, 0))],
        out_specs=pl.BlockSpec((tq, 2 * LANES), lambda b, h, i: (b * nq + i, h)),
        out_shape=jax.ShapeDtypeStruct((t, MLA_HEADS * MLA_V), BF16),
        scratch_shapes=[pltpu.VMEM((2, seq, 2 * LANES), BF16)],
        compiler_params=_cparams(3),
        name="mla_attention",
    )(qn, qpe, kv, kv, mla_in, mla_in, cos2, sin2)


def _sb_body(q_ref, k_ref, v_ref, o_ref, *, tq, tk):
    i = pl.program_id(2)
    row = lax.broadcasted_iota(jnp.int32, (tq, tk), 0)
    col = lax.broadcasted_iota(jnp.int32, (tq, tk), 1)
    krow = lax.broadcasted_iota(jnp.int32, (tk, tk), 0)
    kcol = lax.broadcasted_iota(jnp.int32, (tk, tk), 1)
    later = (krow > kcol).astype(BF16)
    qs = [q_ref[:, hh * LANES:(hh + 1) * LANES] for hh in range(2)]
    per_q = tq // tk

    def block(j, carry, diag):
        r0 = pl.multiple_of(j * tk, tk)
        strict = None if diag is None else (col + diag * tk < row)
        out = []
        for hh in range(2):
            acc, tail_c = carry[hh]
            k = k_ref[pl.ds(r0, tk), hh * LANES:(hh + 1) * LANES]
            v = v_ref[pl.ds(r0, tk), hh * LANES:(hh + 1) * LANES]
            z = lax.dot_general(qs[hh], k, _NT, preferred_element_type=F32)
            sp = jnp.maximum(z, 0.0) + jnp.log(1.0 + jnp.exp(-jnp.abs(z)))
            log_sig = z - sp
            if strict is not None:
                sp = jnp.where(strict, sp, 0.0)
            tail = tail_c - jnp.dot(sp.astype(BF16), later, preferred_element_type=F32)
            a = jnp.exp(log_sig + tail)
            if strict is not None:
                a = jnp.where(strict, a, 0.0)
            acc = acc + jnp.dot(a.astype(BF16), v, preferred_element_type=F32)
            tail_c = tail_c - jnp.sum(sp, axis=1, keepdims=True)
            out.append((acc, tail_c))
        return tuple(out)

    one = (jnp.zeros((tq, SB_DH), F32), jnp.zeros((tq, 1), F32))
    carry = (one, one)
    for d in reversed(range(per_q)):
        carry = block(i * per_q + d, carry, d)
    n_before = i * per_q
    carry = lax.fori_loop(0, n_before, lambda jj, c: block(n_before - 1 - jj, c, None), carry)
    for hh in range(2):
        o_ref[:, hh * LANES:(hh + 1) * LANES] = carry[hh][0].astype(o_ref.dtype)


def _sb_attention(sb_q, sb_kv, *, batch, seq):
    t = sb_q.shape[0]
    tq = _tile(seq, SB_Q_TILE, LANES)
    tk = _tile(tq, SB_K_TILE, LANES)
    nq = seq // tq
    pairs = SB_HEADS // 2
    return pl.pallas_call(
        functools.partial(_sb_body, tq=tq, tk=tk),
        grid=(batch, pairs, nq),
        in_specs=[pl.BlockSpec((tq, 2 * LANES), lambda b, h, i: (b * nq + i, h)),
                  pl.BlockSpec((seq, 2 * LANES), lambda b, h, i: (b, h)),
                  pl.BlockSpec((seq, 2 * LANES), lambda b, h, i: (b, pairs + h))],
        out_specs=pl.BlockSpec((tq, 2 * LANES), lambda b, h, i: (b * nq + i, h)),
        out_shape=jax.ShapeDtypeStruct((t, SB_HEADS * SB_DH), BF16),
        compiler_params=_cparams(3),
        name="sb_attention",
    )(sb_q, sb_kv, sb_kv)


def _offdiag_segments(lo, hi):
    if hi - lo <= SUBCHUNK:
        return []
    mid = (lo + hi) // 2
    return [(lo, mid, hi)] + _offdiag_segments(lo, mid) + _offdiag_segments(mid, hi)


def _pad_rows(piece, lo, hi, total):
    parts = []
    if lo:
        parts.append(jnp.zeros((lo, piece.shape[1]), piece.dtype))
    parts.append(piece)
    if total - hi:
        parts.append(jnp.zeros((total - hi, piece.shape[1]), piece.dtype))
    return jnp.concatenate(parts, axis=0) if len(parts) > 1 else piece


def _hgrn_body(q_ref, f_ref, v_ref, g_ref, loglb_ref, log1mlb_ref, omlb_ref, onorm_ref,
               o_ref, cum_ref, k_ref, vrow_ref, state_ref, *, seq):
    c = CHUNK
    dk = HGRN_DK
    row = lax.broadcasted_iota(jnp.int32, (c, c), 0)
    col = lax.broadcasted_iota(jnp.int32, (c, c), 1)
    tril = (col <= row).astype(BF16)
    tril3 = jnp.concatenate([tril, tril, tril], axis=1)
    sub = lax.broadcasted_iota(jnp.int32, (SUBLANES, dk), 0)
    segs = _offdiag_segments(0, c)
    groups = c // SUBLANES
    groups_per_sub = SUBCHUNK // SUBLANES
    state_ref[...] = jnp.zeros_like(state_ref)

    def one_head(r0, hh, slot):
        sl = slice(hh * dk, (hh + 1) * dk)
        z = f_ref[pl.ds(r0, c), sl]
        log_sig = jnp.minimum(z, 0.0) - jnp.log(1.0 + jnp.exp(-jnp.abs(z)))
        y = log1mlb_ref[:, sl] + log_sig
        log_lb = loglb_ref[:, sl]
        log_f = jnp.maximum(log_lb, y) + jnp.log(1.0 + jnp.exp(-jnp.abs(log_lb - y)))
        kk = omlb_ref[:, sl] * jnp.exp(log_sig - z)
        hi = log_f.astype(BF16)
        rem = log_f - hi.astype(F32)
        mid = rem.astype(BF16)
        lo = (rem - mid.astype(F32)).astype(BF16)
        cum = jnp.dot(tril3, jnp.concatenate([hi, mid, lo], axis=0), preferred_element_type=F32) * LOG2E
        cum_ref[slot] = cum
        k_ref[slot] = kk
        qc = q_ref[pl.ds(r0, c), sl]
        vc = v_ref[pl.ds(r0, c), sl]
        vrow_ref[slot] = vc
        last = cum[c - 1:c, :]
        state_t = state_ref[hh]
        qe = (qc * jnp.exp2(cum)).astype(BF16)
        o_mm = lax.dot_general(qe, state_t.astype(BF16), _NT, preferred_element_type=F32)
        if segs:
            qa, ka = [], []
            for (lo_r, mid_r, hi_r) in segs:
                ref = cum[mid_r - 1:mid_r, :]
                qa.append(_pad_rows(qc[mid_r:hi_r] * jnp.exp2(cum[mid_r:hi_r] - ref), mid_r, hi_r, c))
                ka.append(_pad_rows(kk[lo_r:mid_r] * jnp.exp2(ref - cum[lo_r:mid_r]), lo_r, mid_r, c))
            s_off = lax.dot_general(jnp.concatenate(qa, axis=1).astype(BF16),
                                    jnp.concatenate(ka, axis=1).astype(BF16), _NT,
                                    preferred_element_type=F32)
            o_mm = o_mm + jnp.dot(s_off.astype(BF16), vc.astype(BF16), preferred_element_type=F32)
        o_parts = [o_mm[g * SUBLANES:(g + 1) * SUBLANES] for g in range(groups)]
        for j in range(c):
            kj = k_ref[slot, pl.ds(j, 1), :]
            cj = cum_ref[slot, pl.ds(j, 1), :]
            vj = vrow_ref[slot, pl.ds(j, 1), :]
            g0 = j // SUBLANES
            g_end = (j // SUBCHUNK + 1) * groups_per_sub
            for g in range(g0, g_end):
                qv = qc[g * SUBLANES:(g + 1) * SUBLANES]
                cv = cum[g * SUBLANES:(g + 1) * SUBLANES]
                w = qv * kj * jnp.exp2(cv - cj)
                if g == g0 and j % SUBLANES:
                    w = jnp.where(sub >= j % SUBLANES, w, 0.0)
                o_parts[g] = o_parts[g] + jnp.sum(w, axis=1, keepdims=True) * vj
        o = jnp.concatenate(o_parts, axis=0)
        kd = (kk * jnp.exp2(last - cum)).astype(BF16)
        state_ref[hh] = state_t * jnp.exp2(last) + jnp.dot(vc.T.astype(BF16), kd,
                                                          preferred_element_type=F32)
        o = o * lax.rsqrt(jnp.mean(o * o, axis=-1, keepdims=True) + EPS)
        hg = g_ref[pl.ds(r0, c), sl]
        o = o * onorm_ref[:, sl] * (hg * jax.nn.sigmoid(hg))
        o_ref[pl.ds(r0, c), sl] = o.astype(o_ref.dtype)

    def chunks(ci, _):
        for u in range(HGRN_UNROLL):
            r0 = pl.multiple_of((ci * HGRN_UNROLL + u) * c, c)
            for hh in range(2):
                one_head(r0, hh, 2 * u + hh)
        return 0

    assert seq % (c * HGRN_UNROLL) == 0
    lax.fori_loop(0, seq // (c * HGRN_UNROLL), chunks, 0)


def _hgrn(hg_in, log_lb, log_1mlb, om_lb, out_norm, *, batch, seq):
    t = hg_in.shape[0]
    pairs = HGRN_HEADS // 2
    w = 2 * LANES
    vec = pl.BlockSpec((1, w), lambda b, h: (0, h))
    blk = lambda off: pl.BlockSpec((seq, w), lambda b, h: (b, off * pairs + h))
    return pl.pallas_call(
        functools.partial(_hgrn_body, seq=seq),
        grid=(batch, pairs),
        in_specs=[blk(0), blk(1), blk(2), blk(3), vec, vec, vec, vec],
        out_specs=pl.BlockSpec((seq, w), lambda b, h: (b, h)),
        out_shape=jax.ShapeDtypeStruct((t, HGRN_HEADS * HGRN_DV), BF16),
        scratch_shapes=[pltpu.VMEM((2 * HGRN_UNROLL, CHUNK, HGRN_DK), F32),
                        pltpu.VMEM((2 * HGRN_UNROLL, CHUNK, HGRN_DK), F32),
                        pltpu.VMEM((2 * HGRN_UNROLL, CHUNK, HGRN_DV), F32),
                        pltpu.VMEM((2, HGRN_DV, HGRN_DK), F32)],
        compiler_params=_cparams(2),
        name="hgrn2",
    )(hg_in, hg_in, hg_in, hg_in, log_lb.reshape(1, -1), log_1mlb.reshape(1, -1),
      om_lb.reshape(1, -1), out_norm.reshape(1, -1))


def _merge_body(ya_ref, yb_ref, yc_ref, wa_ref, wb_ref, wc_ref, ga_ref, gb_ref, gc_ref, o_ref):
    acc = ga_ref[...].astype(F32) * jnp.dot(ya_ref[...], wa_ref[...], preferred_element_type=F32)
    acc = acc + gb_ref[...].astype(F32) * jnp.dot(yb_ref[...], wb_ref[...], preferred_element_type=F32)
    acc = acc + gc_ref[...].astype(F32) * jnp.dot(yc_ref[...], wc_ref[...], preferred_element_type=F32)
    o_ref[...] = acc.astype(o_ref.dtype)


def _merge(ya, yb, yc, wa, wb, wc, gates):
    m = ya.shape[0]
    d = wa.shape[1]
    tm = _tile(m, ROW_TILE, SUBLANES)
    tn = _tile(d, COL_TILE // 2, LANES)
    nj = d // tn
    row_blk = lambda arr: pl.BlockSpec((tm, arr.shape[1]), lambda i, j: (i, 0))
    col_blk = lambda arr: pl.BlockSpec((arr.shape[0], tn), lambda i, j: (0, j))
    gate_blk = lambda g: pl.BlockSpec((tm, tn), lambda i, j: (i, g * nj + j))
    return pl.pallas_call(
        _merge_body,
        grid=(m // tm, nj),
        in_specs=[row_blk(ya), row_blk(yb), row_blk(yc), col_blk(wa), col_blk(wb), col_blk(wc),
                  gate_blk(0), gate_blk(1), gate_blk(2)],
        out_specs=pl.BlockSpec((tm, tn), lambda i, j: (i, j)),
        out_shape=jax.ShapeDtypeStruct((m, d), BF16),
        compiler_params=_cparams(2),
        name="branch_merge",
    )(ya, yb, yc, wa, wb, wc, gates, gates, gates)


def _mm_res_body(a_ref, w_ref, res_ref, o_ref):
    kk = pl.program_id(2)
    part = jnp.dot(a_ref[...], w_ref[...], preferred_element_type=F32)

    @pl.when(kk == 0)
    def _():
        o_ref[...] = res_ref[...] + part

    @pl.when(kk != 0)
    def _():
        o_ref[...] += part


def _mm_res(a, w, res, *, name):
    m, k = a.shape
    n = w.shape[1]
    tm = _tile(m, ROW_TILE, SUBLANES)
    tn = _tile(n, COL_TILE // 2, LANES)
    tk = _tile(k, 4 * COL_TILE, LANES)
    return pl.pallas_call(
        _mm_res_body,
        grid=(m // tm, n // tn, k // tk),
        in_specs=[pl.BlockSpec((tm, tk), lambda i, j, kk: (i, kk)),
                  pl.BlockSpec((tk, tn), lambda i, j, kk: (kk, j)),
                  pl.BlockSpec((tm, tn), lambda i, j, kk: (i, j))],
        out_specs=pl.BlockSpec((tm, tn), lambda i, j, kk: (i, j)),
        out_shape=jax.ShapeDtypeStruct((m, n), F32),
        compiler_params=_cparams(3),
        name=name,
    )(a, w, res)


def _ffn_up_body(a_ref, wg_ref, wu_ref, cg_ref, cu_ref, o_ref, stage_ref, *, tm, tn, per_seq):
    i = pl.program_id(1)
    pad = SUBLANES

    @pl.when(i % per_seq == 0)
    def _():
        stage_ref[0:pad, :] = jnp.zeros((pad, 2 * tn), F32)

    @pl.when(i % per_seq != 0)
    def _():
        stage_ref[0:pad, :] = stage_ref[tm:tm + pad, :]

    a = a_ref[...]
    stage_ref[pad:pad + tm, 0:tn] = jnp.dot(a, wg_ref[...], preferred_element_type=F32)
    stage_ref[pad:pad + tm, tn:2 * tn] = jnp.dot(a, wu_ref[...], preferred_element_type=F32)

    def conv(lo, hi, c_ref):
        cw = c_ref[...]
        out = cw[CONV_W - 1:CONV_W, :] * stage_ref[pad:pad + tm, lo:hi]
        for back in range(1, CONV_W):
            out = out + (cw[CONV_W - 1 - back:CONV_W - back, :]
                         * stage_ref[pad - back:pad - back + tm, lo:hi])
        return out

    gate = conv(0, tn, cg_ref)
    up = conv(tn, 2 * tn, cu_ref)
    o_ref[...] = (gate * jax.nn.sigmoid(gate) * up).astype(o_ref.dtype)


def _ffn_up(a, w_up, conv_w, *, seq):
    m, k = a.shape
    d_ff = w_up.shape[1] // 2
    tm = _tile(seq, ROW_TILE, SUBLANES)
    tn = _tile(d_ff, COL_TILE // 2, LANES)
    nj = d_ff // tn
    return pl.pallas_call(
        functools.partial(_ffn_up_body, tm=tm, tn=tn, per_seq=seq // tm),
        grid=(nj, m // tm),
        in_specs=[pl.BlockSpec((tm, k), lambda j, i: (i, 0)),
                  pl.BlockSpec((k, tn), lambda j, i: (0, j)),
                  pl.BlockSpec((k, tn), lambda j, i: (0, nj + j)),
                  pl.BlockSpec((CONV_W, tn), lambda j, i: (0, j)),
                  pl.BlockSpec((CONV_W, tn), lambda j, i: (0, nj + j))],
        out_specs=pl.BlockSpec((tm, tn), lambda j, i: (i, j)),
        out_shape=jax.ShapeDtypeStruct((m, d_ff), BF16),
        scratch_shapes=[pltpu.VMEM((tm + SUBLANES, 2 * tn), F32)],
        compiler_params=_cparams(2),
        name="ffn_up_conv_gate",
    )(a, w_up, w_up, conv_w, conv_w)


def _rot_half_cols(w):
    shape = w.shape
    w = w.reshape(shape[0], -1, 2, MLA_ROPE // 2)
    return jnp.stack([-w[:, :, 1], w[:, :, 0]], axis=2).reshape(shape)


def _split_cols(w, sizes):
    out, c = [], 0
    for s in sizes:
        out.append(w[:, c:c + s])
        c += s
    return out


def kernel(x, w_in, mla_q_norm, mla_kv_norm, mla_w_uq, mla_w_uk, mla_w_uv, hgrn_lb_logits,
           hgrn_out_norm, w_branch_mla, w_branch_hgrn, w_branch_sb, w_out, mix_norm, ffn_norm,
           ffn_w_up, ffn_conv, ffn_w_down, final_norm):
    batch, seq, d = x.shape
    depth = w_in.shape[0]
    t = batch * seq
    q_lora = mla_q_norm.shape[1]
    kv_lora = mla_kv_norm.shape[1]
    hk = HGRN_HEADS * HGRN_DK
    hv = HGRN_HEADS * HGRN_DV
    sbw = SB_HEADS * SB_DH
    qk_dim = MLA_NOPE + MLA_ROPE
    splits = (q_lora, kv_lora, MLA_ROPE, hk, hk, hv, hv, sbw, sbw, sbw, d, d, d)

    inv = 1.0 / (ROPE_THETA ** (jnp.arange(0, MLA_ROPE, 2, dtype=F32) / MLA_ROPE))
    ang = jnp.arange(seq, dtype=F32)[:, None] * inv[None, :]
    cos2 = jnp.tile(jnp.cos(ang), (1, 2 * LANES // MLA_ROPE))
    sin2 = jnp.tile(jnp.sin(ang), (1, 2 * LANES // MLA_ROPE))

    lb_all = jnp.cumsum(jax.nn.softmax(hgrn_lb_logits.astype(F32), axis=0), axis=0)
    lb_all = lb_all - lb_all[0:1]

    xf = x.reshape(t, d)
    for l in range(depth):
        (w_cq, w_ckv, w_kr, w_hq, w_hf, w_hi, w_hg, w_sq, w_sk, w_sv, w_ga, w_gb, w_gc) = _split_cols(w_in[l], splits)
        w_kr_rot = _rot_half_cols(w_kr)
        w_in_x = jnp.concatenate(
            [w_ga, w_gb, w_gc, w_hq, w_hf, w_hi, w_hg, w_sq, w_sk, w_sv,
             w_cq, w_ckv, w_kr, w_kr, w_kr_rot, w_kr_rot], axis=1).astype(BF16)
        n_gate, n_hgrn, n_sb = 3 * d, 2 * hk + 2 * hv, 3 * sbw
        n_mla = q_lora + kv_lora + 2 * LANES
        uq = mla_w_uq[l].reshape(q_lora, MLA_HEADS, qk_dim)
        w_qn = uq[:, :, :MLA_NOPE].reshape(q_lora, MLA_HEADS * MLA_NOPE).astype(BF16)
        w_qpe_f = uq[:, :, MLA_NOPE:].reshape(q_lora, MLA_HEADS * MLA_ROPE)
        w_qpe = w_qpe_f.astype(BF16)
        w_qper = _rot_half_cols(w_qpe_f).astype(BF16)
        w_kv = jnp.concatenate([mla_w_uk[l], mla_w_uv[l]], axis=1).astype(BF16)
        lb = lb_all[l]

        hn = _rmsnorm(xf, mix_norm[l], BF16)
        gates = _mm(hn, w_in_x, k=d, b_col0=0, n=n_gate, out_dtype=BF16, act="sigmoid", name="in_proj_gates")
        hg_in = _mm(hn, w_in_x, k=d, b_col0=n_gate, n=n_hgrn, out_dtype=F32, name="in_proj_hgrn")
        sb_q = _mm(hn, w_in_x, k=d, b_col0=n_gate + n_hgrn, n=sbw, out_dtype=BF16,
                   scale=SB_DH ** -0.5, name="in_proj_sb_q")
        sb_kv = _mm(hn, w_in_x, k=d, b_col0=n_gate + n_hgrn + sbw, n=2 * sbw, out_dtype=BF16,
                    name="in_proj_sb_kv")
        mla_in = _mm(hn, w_in_x, k=d, b_col0=n_gate + n_hgrn + n_sb, n=n_mla, out_dtype=F32,
                     name="in_proj_mla")

        scale = qk_dim ** -0.5 * LOG2E
        qn =_mm(mla_in, w_qn, k=q_lora, a_blk=0, n=w_qn.shape[1], out_dtype=BF16,
                 norm_w=mla_q_norm[l], scale=scale, name="mla_q_nope")
        qpe = _qpe(mla_in, mla_q_norm[l], w_qpe, w_qper, cos2, sin2, k=q_lora, seq=seq, scale=scale)
        kv = _mm(mla_in, w_kv, k=kv_lora, a_blk=q_lora // kv_lora, n=w_kv.shape[1], out_dtype=BF16,
                 norm_w=mla_kv_norm[l], name="mla_kv_up")
        y_a = _mla_attention(qn, qpe, kv, mla_in, cos2, sin2, batch=batch, seq=seq)

        y_b = _hgrn(hg_in, jnp.log(lb), jnp.log1p(-lb), 1.0 - lb, hgrn_out_norm[l], batch=batch, seq=seq)
        y_c = _sb_attention(sb_q, sb_kv, batch=batch, seq=seq)

        merged = _merge(y_a, y_b, y_c, w_branch_mla[l].astype(BF16), w_branch_hgrn[l].astype(BF16),
                        w_branch_sb[l].astype(BF16), gates)
        xf = _mm_res(merged, w_out[l].astype(BF16), xf, name="out_proj_residual")

        hn2 = _rmsnorm(xf, ffn_norm[l], BF16)
        act = _ffn_up(hn2, ffn_w_up[l].astype(BF16), ffn_conv[l], seq=seq)
        xf = _mm_res(act, ffn_w_down[l].astype(BF16), xf, name="ffn_down_residual")

    out = _rmsnorm(xf, final_norm, x.dtype)
    return out.reshape(batch, seq, d)
```

```python
import functools

import numpy as np
import jax
import jax.numpy as jnp
from jax import lax
from jax.experimental import pallas as pl
from jax.experimental.pallas import tpu as pltpu

CHUNK = 64
EPS = 1e-6
MLA_HEADS = 16
MLA_NOPE = 128
MLA_ROPE = 64
MLA_V = 128
ROPE_THETA = 10000.0
HGRN_HEADS = 8
HGRN_DK = 128
HGRN_DV = 128
SB_HEADS = 8
SB_DH = 128
CONV_W = 3
SUBCHUNK = 16
HGRN_UNROLL = 4

LANES = 128
SUBLANES = 8
VMEM_LIMIT_BYTES = 56 * 1024 * 1024
CAST_BLOCK_BYTES = 8 * 1024 * 1024
ROW_TILE = 1024
COL_TILE = 1024
MLA_TILE = 512
SB_Q_TILE = 512
SB_K_TILE = 256
NEG_BIG = -1e30
LOG2E = 1.4426950408889634
assert CHUNK & (CHUNK - 1) == 0

F32 = jnp.float32
BF16 = jnp.bfloat16
_NT = (((1,), (1,)), ((), ()))


def _tile(n, pref, align):
    t = (min(pref, n) // align) * align
    while t >= align:
        if n % t == 0:
            return t
        t -= align
    return n


def _cparams(ndims):
    return pltpu.CompilerParams(dimension_semantics=("arbitrary",) * ndims,
                                vmem_limit_bytes=VMEM_LIMIT_BYTES)


def _rms_rows(x, w):
    ms = jnp.mean(x * x, axis=-1, keepdims=True)
    return x * lax.rsqrt(ms + EPS) * w


def _rmsnorm_body(x_ref, w_ref, o_ref):
    o_ref[...] = _rms_rows(x_ref[...], w_ref[...]).astype(o_ref.dtype)


def _rmsnorm(x2d, w, out_dtype):
    t, d = x2d.shape
    tm = _tile(t, 512, SUBLANES)
    return pl.pallas_call(
        _rmsnorm_body,
        grid=(t // tm,),
        in_specs=[pl.BlockSpec((tm, d), lambda i: (i, 0)),
                  pl.BlockSpec((1, d), lambda i: (0, 0))],
        out_specs=pl.BlockSpec((tm, d), lambda i: (i, 0)),
        out_shape=jax.ShapeDtypeStruct((t, d), out_dtype),
        compiler_params=_cparams(1),
        name="rmsnorm",
    )(x2d, w.reshape(1, d))


def _mm_body(*refs, norm, act, scale):
    if norm:
        a_ref, nw_ref, b_ref, o_ref = refs
        a = _rms_rows(a_ref[...], nw_ref[...]).astype(BF16)
    else:
        a_ref, b_ref, o_ref = refs
        a = a_ref[...]
    acc = jnp.dot(a, b_ref[...], preferred_element_type=F32)
    if act == "sigmoid":
        acc = jax.nn.sigmoid(acc)
    if scale is not None:
        acc = acc * scale
    o_ref[...] = acc.astype(o_ref.dtype)


def _mm(a, b, *, k, a_blk=0, b_col0=0, n, out_dtype, norm_w=None, act=None, scale=None, layer=None,
        name="mm"):
    m = a.shape[0]
    tm = _tile(m, ROW_TILE, SUBLANES)
    tn = _tile(int(np.gcd(n, b_col0)) if b_col0 else n, COL_TILE, LANES)
    assert n % tn == 0 and b_col0 % tn == 0 and b.shape[-2] == k
    j0 = b_col0 // tn
    if layer is None:
        b_spec = pl.BlockSpec((k, tn), lambda i, j: (0, j0 + j))
    else:
        b_spec = pl.BlockSpec((None, k, tn), lambda i, j: (layer, 0, j0 + j))
    in_specs = [pl.BlockSpec((tm, k), lambda i, j: (i, a_blk))]
    args = [a]
    if norm_w is not None:
        in_specs.append(pl.BlockSpec((1, k), lambda i, j: (0, 0)))
        args.append(norm_w.reshape(1, k))
    in_specs.append(b_spec)
    args.append(b)
    return pl.pallas_call(
        functools.partial(_mm_body, norm=norm_w is not None, act=act, scale=scale),
        grid=(m // tm, n // tn),
        in_specs=in_specs,
        out_specs=pl.BlockSpec((tm, tn), lambda i, j: (i, j)),
        out_shape=jax.ShapeDtypeStruct((m, n), out_dtype),
        compiler_params=_cparams(2),
        name=name,
    )(*args)


def _qpe_body(a_ref, nw_ref, w_ref, wr_ref, cos_ref, sin_ref, o_ref, *, scale, reps):
    a = _rms_rows(a_ref[...], nw_ref[...]).astype(BF16)
    p = jnp.dot(a, w_ref[...], preferred_element_type=F32)
    pr = jnp.dot(a, wr_ref[...], preferred_element_type=F32)
    cos = jnp.tile(cos_ref[...], (1, reps))
    sin = jnp.tile(sin_ref[...], (1, reps))
    o_ref[...] = ((p * cos + pr * sin) * scale).astype(o_ref.dtype)


def _qpe(mla_in, norm_w, w_pe, w_per, cos2, sin2, *, k, seq, scale):
    m = mla_in.shape[0]
    n = w_pe.shape[1]
    tm = _tile(seq, ROW_TILE, SUBLANES)
    tn = _tile(n, COL_TILE // 2, LANES)
    per_seq = seq // tm
    return pl.pallas_call(
        functools.partial(_qpe_body, scale=scale, reps=tn // LANES),
        grid=(m // tm, n // tn),
        in_specs=[pl.BlockSpec((tm, k), lambda i, j: (i, 0)),
                  pl.BlockSpec((1, k), lambda i, j: (0, 0)),
                  pl.BlockSpec((k, tn), lambda i, j: (0, j)),
                  pl.BlockSpec((k, tn), lambda i, j: (0, j)),
                  pl.BlockSpec((tm, LANES), lambda i, j: (i % per_seq, 0)),
                  pl.BlockSpec((tm, LANES), lambda i, j: (i % per_seq, 0))],
        out_specs=pl.BlockSpec((tm, tn), lambda i, j: (i, j)),
        out_shape=jax.ShapeDtypeStruct((m, n), BF16),
        compiler_params=_cparams(2),
        name="mla_q_rope",
    )(mla_in, norm_w.reshape(1, k), w_pe, w_per, cos2, sin2)


def _mla_body(qn_ref, qpe_ref, kn_ref, v_ref, kpe_ref, kper_ref, cos_ref, sin_ref,
              o_ref, kcat_ref, *, tq, tk):
    i = pl.program_id(2)

    @pl.when(i == 0)
    def _():
        kpe = (kpe_ref[...] * cos_ref[...] + kper_ref[...] * sin_ref[...]).astype(BF16)
        for hh in range(2):
            kcat_ref[hh, :, :LANES] = kn_ref[:, hh * LANES:(hh + 1) * LANES]
            kcat_ref[hh, :, LANES:] = kpe

    lane = lax.broadcasted_iota(jnp.int32, (tq, LANES), 1)
    qpe = qpe_ref[...]
    zero = jnp.zeros((tq, LANES), BF16)
    qs = [jnp.concatenate([qn_ref[:, 0:LANES], jnp.where(lane < MLA_ROPE, qpe, zero)], axis=1),
          jnp.concatenate([qn_ref[:, LANES:2 * LANES], jnp.where(lane >= MLA_ROPE, qpe, zero)], axis=1)]

    def block(r0, carry, visible):
        out = []
        for hh in range(2):
            m_prev, l_prev, acc = carry[hh]
            k = kcat_ref[hh, pl.ds(r0, tk), :]
            v = v_ref[pl.ds(r0, tk), hh * LANES:(hh + 1) * LANES]
            s = lax.dot_general(qs[hh], k, _NT, preferred_element_type=F32)
            if visible is not None:
                s = jnp.where(visible, s, NEG_BIG)
            m_new = jnp.maximum(m_prev, jnp.max(s, axis=1, keepdims=True))
            alpha = jnp.exp2(m_prev - m_new)
            p = jnp.exp2(s - m_new)
            l_new = alpha * l_prev + jnp.sum(p, axis=1, keepdims=True)
            acc = alpha * acc + jnp.dot(p.astype(BF16), v, preferred_element_type=F32)
            out.append((m_new, l_new, acc))
        return tuple(out)

    one = (jnp.full((tq, 1), NEG_BIG, F32), jnp.zeros((tq, 1), F32), jnp.zeros((tq, MLA_V), F32))
    carry = lax.fori_loop(0, i, lambda j, c: block(pl.multiple_of(j * tk, tk), c, None),
                          (one, one))
    row = lax.broadcasted_iota(jnp.int32, (tq, tk), 0)
    col = lax.broadcasted_iota(jnp.int32, (tq, tk), 1)
    visible = col <= jnp.bitwise_or(row, CHUNK - 1)
    carry = block(pl.multiple_of(i * tk, tk), carry, visible)
    for hh in range(2):
        _, l_fin, acc = carry[hh]
        o_ref[:, hh * LANES:(hh + 1) * LANES] = (acc / l_fin).astype(o_ref.dtype)


def _mla_attention(qn, qpe, kv, mla_in, cos2, sin2, *, batch, seq):
    t = qn.shape[0]
    tq = tk = _tile(seq, MLA_TILE, CHUNK)
    nq = seq // tq
    pairs = MLA_HEADS // 2
    kpe_blk = mla_in.shape[1] // LANES - 2
    return pl.pallas_call(
        functools.partial(_mla_body, tq=tq, tk=tk),
        grid=(batch, pairs, nq),
        in_specs=[pl.BlockSpec((tq, 2 * LANES), lambda b, h, i: (b * nq + i, h)),
                  pl.BlockSpec((tq, LANES), lambda b, h, i: (b * nq + i, h)),
                  pl.BlockSpec((seq, 2 * LANES), lambda b, h, i: (b, h)),
                  pl.BlockSpec((seq, 2 * LANES), lambda b, h, i: (b, pairs + h)),
                  pl.BlockSpec((seq, LANES), lambda b, h, i: (b, kpe_blk)),
                  pl.BlockSpec((seq, LANES), lambda b, h, i: (b, kpe_blk + 1)),
                  pl.BlockSpec((seq, LANES), lambda b, h, i: (0, 0)),
                  pl.BlockSpec((seq, LANES), lambda b, h, i: (0, 0))],
        out_specs=pl.BlockSpec((tq, 2 * LANES), lambda b, h, i: (b * nq + i, h)),
        out_shape=jax.ShapeDtypeStruct((t, MLA_HEADS * MLA_V), BF16),
        scratch_shapes=[pltpu.VMEM((2, seq, 2 * LANES), BF16)],
        compiler_params=_cparams(3),
        name="mla_attention",
    )(qn, qpe, kv, kv, mla_in, mla_in, cos2, sin2)


def _sb_body(q_ref, k_ref, v_ref, o_ref, *, tq, tk):
    i = pl.program_id(2)
    row = lax.broadcasted_iota(jnp.int32, (tq, tk), 0)
    col = lax.broadcasted_iota(jnp.int32, (tq, tk), 1)
    krow = lax.broadcasted_iota(jnp.int32, (tk, tk), 0)
    kcol = lax.broadcasted_iota(jnp.int32, (tk, tk), 1)
    later = (krow > kcol).astype(BF16)
    qs = [q_ref[:, hh * LANES:(hh + 1) * LANES] for hh in range(2)]
    per_q = tq // tk

    def block(j, carry, diag):
        r0 = pl.multiple_of(j * tk, tk)
        strict = None if diag is None else (col + diag * tk < row)
        out = []
        for hh in range(2):
            acc, tail_c = carry[hh]
            k = k_ref[pl.ds(r0, tk), hh * LANES:(hh + 1) * LANES]
            v = v_ref[pl.ds(r0, tk), hh * LANES:(hh + 1) * LANES]
            z = lax.dot_general(qs[hh], k, _NT, preferred_element_type=F32)
            sp = jnp.maximum(z, 0.0) + jnp.log(1.0 + jnp.exp(-jnp.abs(z)))
            log_sig = z - sp
            if strict is not None:
                sp = jnp.where(strict, sp, 0.0)
            tail = tail_c - jnp.dot(sp.astype(BF16), later, preferred_element_type=F32)
            a = jnp.exp(log_sig + tail)
            if strict is not None:
                a = jnp.where(strict, a, 0.0)
            acc = acc + jnp.dot(a.astype(BF16), v, preferred_element_type=F32)
            tail_c = tail_c - jnp.sum(sp, axis=1, keepdims=True)
            out.append((acc, tail_c))
        return tuple(out)

    one = (jnp.zeros((tq, SB_DH), F32), jnp.zeros((tq, 1), F32))
    carry = (one, one)
    for d in reversed(range(per_q)):
        carry = block(i * per_q + d, carry, d)
    n_before = i * per_q
    carry = lax.fori_loop(0, n_before, lambda jj, c: block(n_before - 1 - jj, c, None), carry)
    for hh in range(2):
        o_ref[:, hh * LANES:(hh + 1) * LANES] = carry[hh][0].astype(o_ref.dtype)


def _sb_attention(sb_q, sb_kv, *, batch, seq):
    t = sb_q.shape[0]
    tq = _tile(seq, SB_Q_TILE, LANES)
    tk = _tile(tq, SB_K_TILE, LANES)
    nq = seq // tq
    pairs = SB_HEADS // 2
    return pl.pallas_call(
        functools.partial(_sb_body, tq=tq, tk=tk),
        grid=(batch, pairs, nq),
        in_specs=[pl.BlockSpec((tq, 2 * LANES), lambda b, h, i: (b * nq + i, h)),
                  pl.BlockSpec((seq, 2 * LANES), lambda b, h, i: (b, h)),
                  pl.BlockSpec((seq, 2 * LANES), lambda b, h, i: (b, pairs + h))],
        out_specs=pl.BlockSpec((tq, 2 * LANES), lambda b, h, i: (b * nq + i, h)),
        out_shape=jax.ShapeDtypeStruct((t, SB_HEADS * SB_DH), BF16),
        compiler_params=_cparams(3),
        name="sb_attention",
    )(sb_q, sb_kv, sb_kv)


def _offdiag_segments(lo, hi):
    if hi - lo <= SUBCHUNK:
        return []
    mid = (lo + hi) // 2
    return [(lo, mid, hi)] + _offdiag_segments(lo, mid) + _offdiag_segments(mid, hi)


def _pad_rows(piece, lo, hi, total):
    parts = []
    if lo:
        parts.append(jnp.zeros((lo, piece.shape[1]), piece.dtype))
    parts.append(piece)
    if total - hi:
        parts.append(jnp.zeros((total - hi, piece.shape[1]), piece.dtype))
    return jnp.concatenate(parts, axis=0) if len(parts) > 1 else piece


def _hgrn_body(q_ref, f_ref, v_ref, g_ref, loglb_ref, log1mlb_ref, omlb_ref, onorm_ref,
               o_ref, cum_ref, k_ref, vrow_ref, state_ref, *, seq):
    c = CHUNK
    dk = HGRN_DK
    row = lax.broadcasted_iota(jnp.int32, (c, c), 0)
    col = lax.broadcasted_iota(jnp.int32, (c, c), 1)
    tril = (col <= row).astype(BF16)
    tril3 = jnp.concatenate([tril, tril, tril], axis=1)
    sub = lax.broadcasted_iota(jnp.int32, (SUBLANES, dk), 0)
    segs = _offdiag_segments(0, c)
    groups = c // SUBLANES
    groups_per_sub = SUBCHUNK // SUBLANES
    state_ref[...] = jnp.zeros_like(state_ref)

    def one_head(r0, hh, slot):
        sl = slice(hh * dk, (hh + 1) * dk)
        z = f_ref[pl.ds(r0, c), sl]
        log_sig = jnp.minimum(z, 0.0) - jnp.log(1.0 + jnp.exp(-jnp.abs(z)))
        y = log1mlb_ref[:, sl] + log_sig
        log_lb = loglb_ref[:, sl]
        log_f = jnp.maximum(log_lb, y) + jnp.log(1.0 + jnp.exp(-jnp.abs(log_lb - y)))
        kk = omlb_ref[:, sl] * jnp.exp(log_sig - z)
        hi = log_f.astype(BF16)
        rem = log_f - hi.astype(F32)
        mid = rem.astype(BF16)
        lo = (rem - mid.astype(F32)).astype(BF16)
        cum = jnp.dot(tril3, jnp.concatenate([hi, mid, lo], axis=0), preferred_element_type=F32) * LOG2E
        cum_ref[slot] = cum
        k_ref[slot] = kk
        qc = q_ref[pl.ds(r0, c), sl]
        vc = v_ref[pl.ds(r0, c), sl]
        vrow_ref[slot] = vc
        last = cum[c - 1:c, :]
        state_t = state_ref[hh]
        qe = (qc * jnp.exp2(cum)).astype(BF16)
        o_mm = lax.dot_general(qe, state_t.astype(BF16), _NT, preferred_element_type=F32)
        if segs:
            qa, ka = [], []
            for (lo_r, mid_r, hi_r) in segs:
                ref = cum[mid_r - 1:mid_r, :]
                qa.append(_pad_rows(qc[mid_r:hi_r] * jnp.exp2(cum[mid_r:hi_r] - ref), mid_r, hi_r, c))
                ka.append(_pad_rows(kk[lo_r:mid_r] * jnp.exp2(ref - cum[lo_r:mid_r]), lo_r, mid_r, c))
            s_off = lax.dot_general(jnp.concatenate(qa, axis=1).astype(BF16),
                                    jnp.concatenate(ka, axis=1).astype(BF16), _NT,
                                    preferred_element_type=F32)
            o_mm = o_mm + jnp.dot(s_off.astype(BF16), vc.astype(BF16), preferred_element_type=F32)
        o_parts = [o_mm[g * SUBLANES:(g + 1) * SUBLANES] for g in range(groups)]
        for j in range(c):
            kj = k_ref[slot, pl.ds(j, 1), :]
            cj = cum_ref[slot, pl.ds(j, 1), :]
            vj = vrow_ref[slot, pl.ds(j, 1), :]
            g0 = j // SUBLANES
            g_end = (j // SUBCHUNK + 1) * groups_per_sub
            for g in range(g0, g_end):
                qv = qc[g * SUBLANES:(g + 1) * SUBLANES]
                cv = cum[g * SUBLANES:(g + 1) * SUBLANES]
                w = qv * kj * jnp.exp2(cv - cj)
                if g == g0 and j % SUBLANES:
                    w = jnp.where(sub >= j % SUBLANES, w, 0.0)
                o_parts[g] = o_parts[g] + jnp.sum(w, axis=1, keepdims=True) * vj
        o = jnp.concatenate(o_parts, axis=0)
        kd = (kk * jnp.exp2(last - cum)).astype(BF16)
        state_ref[hh] = state_t * jnp.exp2(last) + jnp.dot(vc.T.astype(BF16), kd,
                                                          preferred_element_type=F32)
        o = o * lax.rsqrt(jnp.mean(o * o, axis=-1, keepdims=True) + EPS)
        hg = g_ref[pl.ds(r0, c), sl]
        o = o * onorm_ref[:, sl] * (hg * jax.nn.sigmoid(hg))
        o_ref[pl.ds(r0, c), sl] = o.astype(o_ref.dtype)

    def chunks(ci, _):
        for u in range(HGRN_UNROLL):
            r0 = pl.multiple_of((ci * HGRN_UNROLL + u) * c, c)
            for hh in range(2):
                one_head(r0, hh, 2 * u + hh)
        return 0

    assert seq % (c * HGRN_UNROLL) == 0
    lax.fori_loop(0, seq // (c * HGRN_UNROLL), chunks, 0)


def _hgrn(hg_in, log_lb, log_1mlb, om_lb, out_norm, *, batch, seq):
    t = hg_in.shape[0]
    pairs = HGRN_HEADS // 2
    w = 2 * LANES
    vec = pl.BlockSpec((1, w), lambda b, h: (0, h))
    blk = lambda off: pl.BlockSpec((seq, w), lambda b, h: (b, off * pairs + h))
    return pl.pallas_call(
        functools.partial(_hgrn_body, seq=seq),
        grid=(batch, pairs),
        in_specs=[blk(0), blk(1), blk(2), blk(3), vec, vec, vec, vec],
        out_specs=pl.BlockSpec((seq, w), lambda b, h: (b, h)),
        out_shape=jax.ShapeDtypeStruct((t, HGRN_HEADS * HGRN_DV), BF16),
        scratch_shapes=[pltpu.VMEM((2 * HGRN_UNROLL, CHUNK, HGRN_DK), F32),
                        pltpu.VMEM((2 * HGRN_UNROLL, CHUNK, HGRN_DK), F32),
                        pltpu.VMEM((2 * HGRN_UNROLL, CHUNK, HGRN_DV), F32),
                        pltpu.VMEM((2, HGRN_DV, HGRN_DK), F32)],
        compiler_params=_cparams(2),
        name="hgrn2",
    )(hg_in, hg_in, hg_in, hg_in, log_lb.reshape(1, -1), log_1mlb.reshape(1, -1),
      om_lb.reshape(1, -1), out_norm.reshape(1, -1))


def _merge_body(ya_ref, yb_ref, yc_ref, wa_ref, wb_ref, wc_ref, ga_ref, gb_ref, gc_ref, o_ref):
    acc = ga_ref[...].astype(F32) * jnp.dot(ya_ref[...], wa_ref[...], preferred_element_type=F32)
    acc = acc + gb_ref[...].astype(F32) * jnp.dot(yb_ref[...], wb_ref[...], preferred_element_type=F32)
    acc = acc + gc_ref[...].astype(F32) * jnp.dot(yc_ref[...], wc_ref[...], preferred_element_type=F32)
    o_ref[...] = acc.astype(o_ref.dtype)


def _merge(ya, yb, yc, wa, wb, wc, gates, layer):
    m = ya.shape[0]
    d = wa.shape[2]
    tm = _tile(m, ROW_TILE, SUBLANES)
    tn = _tile(d, COL_TILE // 2, LANES)
    nj = d // tn
    row_blk = lambda arr: pl.BlockSpec((tm, arr.shape[1]), lambda i, j: (i, 0))
    col_blk = lambda arr: pl.BlockSpec((None, arr.shape[1], tn), lambda i, j: (layer, 0, j))
    gate_blk = lambda g: pl.BlockSpec((tm, tn), lambda i, j: (i, g * nj + j))
    return pl.pallas_call(
        _merge_body,
        grid=(m // tm, nj),
        in_specs=[row_blk(ya), row_blk(yb), row_blk(yc), col_blk(wa), col_blk(wb), col_blk(wc),
                  gate_blk(0), gate_blk(1), gate_blk(2)],
        out_specs=pl.BlockSpec((tm, tn), lambda i, j: (i, j)),
        out_shape=jax.ShapeDtypeStruct((m, d), BF16),
        compiler_params=_cparams(2),
        name="branch_merge",
    )(ya, yb, yc, wa, wb, wc, gates, gates, gates)


def _mm_res_body(a_ref, w_ref, res_ref, o_ref):
    kk = pl.program_id(2)
    part = jnp.dot(a_ref[...], w_ref[...], preferred_element_type=F32)

    @pl.when(kk == 0)
    def _():
        o_ref[...] = res_ref[...] + part

    @pl.when(kk != 0)
    def _():
        o_ref[...] += part


def _mm_res(a, w, res, layer, *, name):
    m, k = a.shape
    n = w.shape[2]
    tm = _tile(m, ROW_TILE, SUBLANES)
    tn = _tile(n, COL_TILE // 2, LANES)
    tk = _tile(k, 4 * COL_TILE, LANES)
    return pl.pallas_call(
        _mm_res_body,
        grid=(m // tm, n // tn, k // tk),
        in_specs=[pl.BlockSpec((tm, tk), lambda i, j, kk: (i, kk)),
                  pl.BlockSpec((None, tk, tn), lambda i, j, kk: (layer, kk, j)),
                  pl.BlockSpec((tm, tn), lambda i, j, kk: (i, j))],
        out_specs=pl.BlockSpec((tm, tn), lambda i, j, kk: (i, j)),
        out_shape=jax.ShapeDtypeStruct((m, n), F32),
        compiler_params=_cparams(3),
        name=name,
    )(a, w, res)


def _ffn_up_body(a_ref, wg_ref, wu_ref, cg_ref, cu_ref, o_ref, stage_ref, *, tm, tn, per_seq):
    i = pl.program_id(1)
    pad = SUBLANES

    @pl.when(i % per_seq == 0)
    def _():
        stage_ref[0:pad, :] = jnp.zeros((pad, 2 * tn), F32)

    @pl.when(i % per_seq != 0)
    def _():
        stage_ref[0:pad, :] = stage_ref[tm:tm + pad, :]

    a = a_ref[...]
    stage_ref[pad:pad + tm, 0:tn] = jnp.dot(a, wg_ref[...], preferred_element_type=F32)
    stage_ref[pad:pad + tm, tn:2 * tn] = jnp.dot(a, wu_ref[...], preferred_element_type=F32)

    def conv(lo, hi, c_ref):
        cw = c_ref[...]
        out = cw[CONV_W - 1:CONV_W, :] * stage_ref[pad:pad + tm, lo:hi]
        for back in range(1, CONV_W):
            out = out + (cw[CONV_W - 1 - back:CONV_W - back, :]
                         * stage_ref[pad - back:pad - back + tm, lo:hi])
        return out

    gate = conv(0, tn, cg_ref)
    up = conv(tn, 2 * tn, cu_ref)
    o_ref[...] = (gate * jax.nn.sigmoid(gate) * up).astype(o_ref.dtype)


def _ffn_up(a, w_up, conv_w, layer, *, seq):
    m, k = a.shape
    d_ff = w_up.shape[2] // 2
    tm = _tile(seq, ROW_TILE, SUBLANES)
    tn = _tile(d_ff, COL_TILE // 2, LANES)
    nj = d_ff // tn
    return pl.pallas_call(
        functools.partial(_ffn_up_body, tm=tm, tn=tn, per_seq=seq // tm),
        grid=(nj, m // tm),
        in_specs=[pl.BlockSpec((tm, k), lambda j, i: (i, 0)),
                  pl.BlockSpec((None, k, tn), lambda j, i: (layer, 0, j)),
                  pl.BlockSpec((None, k, tn), lambda j, i: (layer, 0, nj + j)),
                  pl.BlockSpec((CONV_W, tn), lambda j, i: (0, j)),
                  pl.BlockSpec((CONV_W, tn), lambda j, i: (0, nj + j))],
        out_specs=pl.BlockSpec((tm, tn), lambda j, i: (i, j)),
        out_shape=jax.ShapeDtypeStruct((m, d_ff), BF16),
        scratch_shapes=[pltpu.VMEM((tm + SUBLANES, 2 * tn), F32)],
        compiler_params=_cparams(2),
        name="ffn_up_conv_gate",
    )(a, w_up, w_up, conv_w, conv_w)


def _cast_body(x_ref, o_ref):
    o_ref[...] = x_ref[...].astype(o_ref.dtype)


def _cast_bf16(w):
    nl, k, n = w.shape
    rk = _tile(k, max(2 * SUBLANES, CAST_BLOCK_BYTES // (4 * n)), 2 * SUBLANES)
    return pl.pallas_call(
        _cast_body,
        grid=(nl, k // rk),
        in_specs=[pl.BlockSpec((None, rk, n), lambda l, r: (l, r, 0))],
        out_specs=pl.BlockSpec((None, rk, n), lambda l, r: (l, r, 0)),
        out_shape=jax.ShapeDtypeStruct(w.shape, BF16),
        compiler_params=_cparams(2),
        name="weight_cast",
    )(w)


def _w_in_body(lo_ref, hi_ref, extra_ref, o_ref, *, n_shift, n_plain, mis):
    j = pl.program_id(2)
    w = lo_ref.shape[1]

    @pl.when(j < n_shift)
    def _():
        lane = lax.broadcasted_iota(jnp.int32, (lo_ref.shape[0], LANES), 1)
        for m in range(w // LANES):
            cur = lo_ref[:, m * LANES:(m + 1) * LANES]
            nxt = hi_ref[...] if (m + 1) * LANES == w else lo_ref[:, (m + 1) * LANES:(m + 2) * LANES]
            t = jnp.where(lane >= mis, cur, nxt)
            o_ref[:, m * LANES:(m + 1) * LANES] = pltpu.roll(t, LANES - mis, axis=1).astype(o_ref.dtype)

    @pl.when(jnp.logical_and(j >= n_shift, j < n_shift + n_plain))
    def _():
        o_ref[...] = lo_ref[...].astype(o_ref.dtype)

    @pl.when(j >= n_shift + n_plain)
    def _():
        o_ref[...] = extra_ref[...]


def _w_in_relayout(w_in, extra, *, n_plain_cols, n_skip_cols):
    nl, d, n_in = w_in.shape
    shift_src = n_plain_cols + n_skip_cols
    mis = shift_src % LANES
    assert 0 < mis < LANES
    start = shift_src - mis
    n_shift_cols = n_in - shift_src
    w = _tile(int(np.gcd(np.gcd(start, n_shift_cols), n_plain_cols)), 512, LANES)
    assert start % w == 0 and n_shift_cols % w == 0 and n_plain_cols % w == 0
    n_shift, n_plain = n_shift_cols // w, n_plain_cols // w
    n_extra = -(-extra.shape[2] // w)
    extra = jnp.pad(extra, ((0, 0), (0, 0), (0, n_extra * w - extra.shape[2])))
    rows = _tile(d, 2048, 2 * SUBLANES)
    lo_idx = lambda j: jnp.where(j < n_shift, start // w + j,
                                 jnp.where(j < n_shift + n_plain, j - n_shift, 0))
    hi_idx = lambda j: jnp.where(j < n_shift, (start + w * (j + 1)) // LANES, 0)
    ex_idx = lambda j: jnp.where(j >= n_shift + n_plain, j - n_shift - n_plain, 0)
    out = pl.pallas_call(
        functools.partial(_w_in_body, n_shift=n_shift, n_plain=n_plain, mis=mis),
        grid=(nl, d // rows, n_shift + n_plain + n_extra),
        in_specs=[pl.BlockSpec((None, rows, w), lambda l, r, j: (l, r, lo_idx(j))),
                  pl.BlockSpec((None, rows, LANES), lambda l, r, j: (l, r, hi_idx(j))),
                  pl.BlockSpec((None, rows, w), lambda l, r, j: (l, r, ex_idx(j)))],
        out_specs=pl.BlockSpec((None, rows, w), lambda l, r, j: (l, r, j)),
        out_shape=jax.ShapeDtypeStruct((nl, d, (n_shift + n_plain + n_extra) * w), BF16),
        compiler_params=_cparams(3),
        name="w_in_relayout",
    )(w_in, w_in, extra)
    return out


def _rot_half_cols(w):
    shape = w.shape
    w = w.reshape(shape[0], -1, 2, MLA_ROPE // 2)
    return jnp.stack([-w[:, :, 1], w[:, :, 0]], axis=2).reshape(shape)


def kernel(x, w_in, mla_q_norm, mla_kv_norm, mla_w_uq, mla_w_uk, mla_w_uv, hgrn_lb_logits,
           hgrn_out_norm, w_branch_mla, w_branch_hgrn, w_branch_sb, w_out, mix_norm, ffn_norm,
           ffn_w_up, ffn_conv, ffn_w_down, final_norm):
    batch, seq, d = x.shape
    depth = w_in.shape[0]
    t = batch * seq
    q_lora = mla_q_norm.shape[1]
    kv_lora = mla_kv_norm.shape[1]
    hk = HGRN_HEADS * HGRN_DK
    hv = HGRN_HEADS * HGRN_DV
    sbw = SB_HEADS * SB_DH
    qk_dim = MLA_NOPE + MLA_ROPE
    splits = (q_lora, kv_lora, MLA_ROPE, hk, hk, hv, hv, sbw, sbw, sbw, d, d, d)

    inv = 1.0 / (ROPE_THETA ** (jnp.arange(0, MLA_ROPE, 2, dtype=F32) / MLA_ROPE))
    ang = jnp.arange(seq, dtype=F32)[:, None] * inv[None, :]
    cos2 = jnp.tile(jnp.cos(ang), (1, 2 * LANES // MLA_ROPE))
    sin2 = jnp.tile(jnp.sin(ang), (1, 2 * LANES // MLA_ROPE))

    lb_all = jnp.cumsum(jax.nn.softmax(hgrn_lb_logits.astype(F32), axis=0), axis=0)
    lb_all = lb_all - lb_all[0:1]

    wb_mla, wb_hgrn, wb_sb = _cast_bf16(w_branch_mla), _cast_bf16(w_branch_hgrn), _cast_bf16(w_branch_sb)
    w_out_b, w_up_b, w_down_b = _cast_bf16(w_out), _cast_bf16(ffn_w_up), _cast_bf16(ffn_w_down)

    n_lat = q_lora + kv_lora
    w_kr = w_in[:, :, n_lat:n_lat + MLA_ROPE].reshape(depth * d, MLA_ROPE)
    w_kr_rot = _rot_half_cols(w_kr)
    kr4 = jnp.concatenate([w_kr, w_kr, w_kr_rot, w_kr_rot], axis=1).astype(BF16).reshape(depth, d, 2 * LANES)
    w_in_x = _w_in_relayout(w_in, kr4, n_plain_cols=n_lat, n_skip_cols=MLA_ROPE)
    n_hgrn, n_sb, n_gate = 2 * hk + 2 * hv, 3 * sbw, 3 * d
    n_mla = n_lat + 2 * LANES
    c_sb, c_gate, c_mla = n_hgrn, n_hgrn + n_sb, n_hgrn + n_sb + n_gate
    assert sum(splits) == w_in.shape[2]

    xf = x.reshape(t, d)
    for l in range(depth):
        uq = mla_w_uq[l].reshape(q_lora, MLA_HEADS, qk_dim)
        w_qn = uq[:, :, :MLA_NOPE].reshape(q_lora, MLA_HEADS * MLA_NOPE).astype(BF16)
        w_qpe_f = uq[:, :, MLA_NOPE:].reshape(q_lora, MLA_HEADS * MLA_ROPE)
        w_qpe = w_qpe_f.astype(BF16)
        w_qper = _rot_half_cols(w_qpe_f).astype(BF16)
        w_kv = jnp.concatenate([mla_w_uk[l], mla_w_uv[l]], axis=1).astype(BF16)
        lb = lb_all[l]

        hn = _rmsnorm(xf, mix_norm[l], BF16)
        gates = _mm(hn, w_in_x, k=d, b_col0=c_gate, n=n_gate, out_dtype=BF16, act="sigmoid", layer=l,
                    name="in_proj_gates")
        hg_in = _mm(hn, w_in_x, k=d, b_col0=0, n=n_hgrn, out_dtype=F32, layer=l, name="in_proj_hgrn")
        sb_q = _mm(hn, w_in_x, k=d, b_col0=c_sb, n=sbw, out_dtype=BF16, scale=SB_DH ** -0.5, layer=l,
                   name="in_proj_sb_q")
        sb_kv = _mm(hn, w_in_x, k=d, b_col0=c_sb + sbw, n=2 * sbw, out_dtype=BF16, layer=l,
                    name="in_proj_sb_kv")
        mla_in = _mm(hn, w_in_x, k=d, b_col0=c_mla, n=n_mla, out_dtype=F32, layer=l, name="in_proj_mla")

        scale = qk_dim ** -0.5 * LOG2E
        qn =_mm(mla_in, w_qn, k=q_lora, a_blk=0, n=w_qn.shape[1], out_dtype=BF16,
                 norm_w=mla_q_norm[l], scale=scale, name="mla_q_nope")
        qpe = _qpe(mla_in, mla_q_norm[l], w_qpe, w_qper, cos2, sin2, k=q_lora, seq=seq, scale=scale)
        kv = _mm(mla_in, w_kv, k=kv_lora, a_blk=q_lora // kv_lora, n=w_kv.shape[1], out_dtype=BF16,
                 norm_w=mla_kv_norm[l], name="mla_kv_up")
        y_a = _mla_attention(qn, qpe, kv, mla_in, cos2, sin2, batch=batch, seq=seq)

        y_b = _hgrn(hg_in, jnp.log(lb), jnp.log1p(-lb), 1.0 - lb, hgrn_out_norm[l], batch=batch, seq=seq)
        y_c = _sb_attention(sb_q, sb_kv, batch=batch, seq=seq)

        merged = _merge(y_a, y_b, y_c, wb_mla, wb_hgrn, wb_sb, gates, l)
        xf = _mm_res(merged, w_out_b, xf, l, name="out_proj_residual")

        hn2 = _rmsnorm(xf, ffn_norm[l], BF16)
        act = _ffn_up(hn2, w_up_b, ffn_conv[l], l, seq=seq)
        xf = _mm_res(act, w_down_b, xf, l, name="ffn_down_residual")

    out = _rmsnorm(xf, final_norm, x.dtype)
    return out.reshape(batch, seq, d)
```

```python
import functools

import numpy as np
import jax
import jax.numpy as jnp
from jax import lax
from jax.experimental import pallas as pl
from jax.experimental.pallas import tpu as pltpu

CHUNK = 64
EPS = 1e-6
MLA_HEADS = 16
MLA_NOPE = 128
MLA_ROPE = 64
MLA_V = 128
ROPE_THETA = 10000.0
HGRN_HEADS = 8
HGRN_DK = 128
HGRN_DV = 128
SB_HEADS = 8
SB_DH = 128
CONV_W = 3
SUBCHUNK = 16
HGRN_UNROLL = 4

LANES = 128
SUBLANES = 8
VMEM_LIMIT_BYTES = 56 * 1024 * 1024
CAST_BLOCK_BYTES = 8 * 1024 * 1024
ROW_TILE = 1024
COL_TILE = 1024
MLA_TILE = 512
SB_Q_TILE = 512
SB_K_TILE = 256
NEG_BIG = -1e30
LOG2E = 1.4426950408889634
assert CHUNK & (CHUNK - 1) == 0

F32 = jnp.float32
BF16 = jnp.bfloat16
_NT = (((1,), (1,)), ((), ()))


def _tile(n, pref, align):
    t = (min(pref, n) // align) * align
    while t >= align:
        if n % t == 0:
            return t
        t -= align
    return n


def _cparams(ndims):
    return pltpu.CompilerParams(dimension_semantics=("arbitrary",) * ndims,
                                vmem_limit_bytes=VMEM_LIMIT_BYTES)


def _rms_rows(x, w):
    ms = jnp.mean(x * x, axis=-1, keepdims=True)
    return x * lax.rsqrt(ms + EPS) * w


def _rmsnorm_body(x_ref, w_ref, o_ref):
    o_ref[...] = _rms_rows(x_ref[...], w_ref[...]).astype(o_ref.dtype)


def _rmsnorm(x2d, w, out_dtype):
    t, d = x2d.shape
    tm = _tile(t, 512, SUBLANES)
    return pl.pallas_call(
        _rmsnorm_body,
        grid=(t // tm,),
        in_specs=[pl.BlockSpec((tm, d), lambda i: (i, 0)),
                  pl.BlockSpec((1, d), lambda i: (0, 0))],
        out_specs=pl.BlockSpec((tm, d), lambda i: (i, 0)),
        out_shape=jax.ShapeDtypeStruct((t, d), out_dtype),
        compiler_params=_cparams(1),
        name="rmsnorm",
    )(x2d, w.reshape(1, d))


def _mm_body(*refs, norm, act, scale, b_rows):
    if norm:
        a_ref, nw_ref, b_ref, o_ref = refs
        a = _rms_rows(a_ref[...], nw_ref[...]).astype(BF16)
    else:
        a_ref, b_ref, o_ref = refs
        a = a_ref[...]
    if b_rows:
        acc = lax.dot_general(a, b_ref[...], _NT, preferred_element_type=F32)
    else:
        acc = jnp.dot(a, b_ref[...], preferred_element_type=F32)
    if act == "sigmoid":
        acc = jax.nn.sigmoid(acc)
    if scale is not None:
        acc = acc * scale
    o_ref[...] = acc.astype(o_ref.dtype)


def _mm(a, b, *, k, a_blk=0, b_col0=0, n, out_dtype, norm_w=None, act=None, scale=None, layer=None,
        b_rows=False, name="mm"):
    m = a.shape[0]
    tm = _tile(m, ROW_TILE, SUBLANES)
    tn = _tile(int(np.gcd(n, b_col0)) if b_col0 else n, COL_TILE, LANES)
    assert n % tn == 0 and b_col0 % tn == 0 and b.shape[-1 if b_rows else -2] == k
    j0 = b_col0 // tn
    if b_rows:
        b_spec = pl.BlockSpec((None, tn, k), lambda i, j: (layer, j0 + j, 0))
    elif layer is None:
        b_spec = pl.BlockSpec((k, tn), lambda i, j: (0, j0 + j))
    else:
        b_spec = pl.BlockSpec((None, k, tn), lambda i, j: (layer, 0, j0 + j))
    in_specs = [pl.BlockSpec((tm, k), lambda i, j: (i, a_blk))]
    args = [a]
    if norm_w is not None:
        in_specs.append(pl.BlockSpec((1, k), lambda i, j: (0, 0)))
        args.append(norm_w.reshape(1, k))
    in_specs.append(b_spec)
    args.append(b)
    return pl.pallas_call(
        functools.partial(_mm_body, norm=norm_w is not None, act=act, scale=scale, b_rows=b_rows),
        grid=(m // tm, n // tn),
        in_specs=in_specs,
        out_specs=pl.BlockSpec((tm, tn), lambda i, j: (i, j)),
        out_shape=jax.ShapeDtypeStruct((m, n), out_dtype),
        compiler_params=_cparams(2),
        name=name,
    )(*args)


def _qpe_body(a_ref, nw_ref, w_ref, wr_ref, cos_ref, sin_ref, o_ref, *, scale, reps):
    a = _rms_rows(a_ref[...], nw_ref[...]).astype(BF16)
    p = jnp.dot(a, w_ref[...], preferred_element_type=F32)
    pr = jnp.dot(a, wr_ref[...], preferred_element_type=F32)
    cos = jnp.tile(cos_ref[...], (1, reps))
    sin = jnp.tile(sin_ref[...], (1, reps))
    o_ref[...] = ((p * cos + pr * sin) * scale).astype(o_ref.dtype)


def _qpe(mla_in, norm_w, w_pe, w_per, cos2, sin2, *, k, seq, scale):
    m = mla_in.shape[0]
    n = w_pe.shape[1]
    tm = _tile(seq, ROW_TILE, SUBLANES)
    tn = _tile(n, COL_TILE // 2, LANES)
    per_seq = seq // tm
    return pl.pallas_call(
        functools.partial(_qpe_body, scale=scale, reps=tn // LANES),
        grid=(m // tm, n // tn),
        in_specs=[pl.BlockSpec((tm, k), lambda i, j: (i, 0)),
                  pl.BlockSpec((1, k), lambda i, j: (0, 0)),
                  pl.BlockSpec((k, tn), lambda i, j: (0, j)),
                  pl.BlockSpec((k, tn), lambda i, j: (0, j)),
                  pl.BlockSpec((tm, LANES), lambda i, j: (i % per_seq, 0)),
                  pl.BlockSpec((tm, LANES), lambda i, j: (i % per_seq, 0))],
        out_specs=pl.BlockSpec((tm, tn), lambda i, j: (i, j)),
        out_shape=jax.ShapeDtypeStruct((m, n), BF16),
        compiler_params=_cparams(2),
        name="mla_q_rope",
    )(mla_in, norm_w.reshape(1, k), w_pe, w_per, cos2, sin2)


def _mla_body(qn_ref, qpe_ref, kn_ref, v_ref, kpe_ref, kper_ref, cos_ref, sin_ref,
              o_ref, kcat_ref, *, tq, tk):
    i = pl.program_id(2)

    @pl.when(i == 0)
    def _():
        kpe = (kpe_ref[...] * cos_ref[...] + kper_ref[...] * sin_ref[...]).astype(BF16)
        for hh in range(2):
            kcat_ref[hh, :, :LANES] = kn_ref[:, hh * LANES:(hh + 1) * LANES]
            kcat_ref[hh, :, LANES:] = kpe

    lane = lax.broadcasted_iota(jnp.int32, (tq, LANES), 1)
    qpe = qpe_ref[...]
    zero = jnp.zeros((tq, LANES), BF16)
    qs = [jnp.concatenate([qn_ref[:, 0:LANES], jnp.where(lane < MLA_ROPE, qpe, zero)], axis=1),
          jnp.concatenate([qn_ref[:, LANES:2 * LANES], jnp.where(lane >= MLA_ROPE, qpe, zero)], axis=1)]

    def block(r0, carry, visible):
        out = []
        for hh in range(2):
            m_prev, l_prev, acc = carry[hh]
            k = kcat_ref[hh, pl.ds(r0, tk), :]
            v = v_ref[pl.ds(r0, tk), hh * LANES:(hh + 1) * LANES]
            s = lax.dot_general(qs[hh], k, _NT, preferred_element_type=F32)
            if visible is not None:
                s = jnp.where(visible, s, NEG_BIG)
            m_new = jnp.maximum(m_prev, jnp.max(s, axis=1, keepdims=True))
            alpha = jnp.exp2(m_prev - m_new)
            p = jnp.exp2(s - m_new)
            l_new = alpha * l_prev + jnp.sum(p, axis=1, keepdims=True)
            acc = alpha * acc + jnp.dot(p.astype(BF16), v, preferred_element_type=F32)
            out.append((m_new, l_new, acc))
        return tuple(out)

    one = (jnp.full((tq, 1), NEG_BIG, F32), jnp.zeros((tq, 1), F32), jnp.zeros((tq, MLA_V), F32))
    carry = lax.fori_loop(0, i, lambda j, c: block(pl.multiple_of(j * tk, tk), c, None),
                          (one, one))
    row = lax.broadcasted_iota(jnp.int32, (tq, tk), 0)
    col = lax.broadcasted_iota(jnp.int32, (tq, tk), 1)
    visible = col <= jnp.bitwise_or(row, CHUNK - 1)
    carry = block(pl.multiple_of(i * tk, tk), carry, visible)
    for hh in range(2):
        _, l_fin, acc = carry[hh]
        o_ref[:, hh * LANES:(hh + 1) * LANES] = (acc / l_fin).astype(o_ref.dtype)


def _mla_attention(qn, qpe, kv, mla_in, cos2, sin2, *, batch, seq):
    t = qn.shape[0]
    tq = tk = _tile(seq, MLA_TILE, CHUNK)
    nq = seq // tq
    pairs = MLA_HEADS // 2
    kpe_blk = mla_in.shape[1] // LANES - 2
    return pl.pallas_call(
        functools.partial(_mla_body, tq=tq, tk=tk),
        grid=(batch, pairs, nq),
        in_specs=[pl.BlockSpec((tq, 2 * LANES), lambda b, h, i: (b * nq + i, h)),
                  pl.BlockSpec((tq, LANES), lambda b, h, i: (b * nq + i, h)),
                  pl.BlockSpec((seq, 2 * LANES), lambda b, h, i: (b, h)),
                  pl.BlockSpec((seq, 2 * LANES), lambda b, h, i: (b, pairs + h)),
                  pl.BlockSpec((seq, LANES), lambda b, h, i: (b, kpe_blk)),
                  pl.BlockSpec((seq, LANES), lambda b, h, i: (b, kpe_blk + 1)),
                  pl.BlockSpec((seq, LANES), lambda b, h, i: (0, 0)),
                  pl.BlockSpec((seq, LANES), lambda b, h, i: (0, 0))],
        out_specs=pl.BlockSpec((tq, 2 * LANES), lambda b, h, i: (b * nq + i, h)),
        out_shape=jax.ShapeDtypeStruct((t, MLA_HEADS * MLA_V), BF16),
        scratch_shapes=[pltpu.VMEM((2, seq, 2 * LANES), BF16)],
        compiler_params=_cparams(3),
        name="mla_attention",
    )(qn, qpe, kv, kv, mla_in, mla_in, cos2, sin2)


def _sb_body(q_ref, k_ref, v_ref, o_ref, *, tq, tk):
    i = pl.program_id(2)
    row = lax.broadcasted_iota(jnp.int32, (tq, tk), 0)
    col = lax.broadcasted_iota(jnp.int32, (tq, tk), 1)
    krow = lax.broadcasted_iota(jnp.int32, (tk, tk), 0)
    kcol = lax.broadcasted_iota(jnp.int32, (tk, tk), 1)
    later = (krow > kcol).astype(BF16)
    qs = [q_ref[:, hh * LANES:(hh + 1) * LANES] for hh in range(2)]
    per_q = tq // tk

    def block(j, carry, diag):
        r0 = pl.multiple_of(j * tk, tk)
        strict = None if diag is None else (col + diag * tk < row)
        out = []
        for hh in range(2):
            acc, tail_c = carry[hh]
            k = k_ref[pl.ds(r0, tk), hh * LANES:(hh + 1) * LANES]
            v = v_ref[pl.ds(r0, tk), hh * LANES:(hh + 1) * LANES]
            z = lax.dot_general(qs[hh], k, _NT, preferred_element_type=F32)
            sp = jnp.maximum(z, 0.0) + jnp.log(1.0 + jnp.exp(-jnp.abs(z)))
            log_sig = z - sp
            if strict is not None:
                sp = jnp.where(strict, sp, 0.0)
            tail = tail_c - jnp.dot(sp.astype(BF16), later, preferred_element_type=F32)
            a = jnp.exp(log_sig + tail)
            if strict is not None:
                a = jnp.where(strict, a, 0.0)
            acc = acc + jnp.dot(a.astype(BF16), v, preferred_element_type=F32)
            tail_c = tail_c - jnp.sum(sp, axis=1, keepdims=True)
            out.append((acc, tail_c))
        return tuple(out)

    one = (jnp.zeros((tq, SB_DH), F32), jnp.zeros((tq, 1), F32))
    carry = (one, one)
    for d in reversed(range(per_q)):
        carry = block(i * per_q + d, carry, d)
    n_before = i * per_q
    carry = lax.fori_loop(0, n_before, lambda jj, c: block(n_before - 1 - jj, c, None), carry)
    for hh in range(2):
        o_ref[:, hh * LANES:(hh + 1) * LANES] = carry[hh][0].astype(o_ref.dtype)


def _sb_attention(sb_q, sb_kv, *, batch, seq):
    t = sb_q.shape[0]
    tq = _tile(seq, SB_Q_TILE, LANES)
    tk = _tile(tq, SB_K_TILE, LANES)
    nq = seq // tq
    pairs = SB_HEADS // 2
    return pl.pallas_call(
        functools.partial(_sb_body, tq=tq, tk=tk),
        grid=(batch, pairs, nq),
        in_specs=[pl.BlockSpec((tq, 2 * LANES), lambda b, h, i: (b * nq + i, h)),
                  pl.BlockSpec((seq, 2 * LANES), lambda b, h, i: (b, h)),
                  pl.BlockSpec((seq, 2 * LANES), lambda b, h, i: (b, pairs + h))],
        out_specs=pl.BlockSpec((tq, 2 * LANES), lambda b, h, i: (b * nq + i, h)),
        out_shape=jax.ShapeDtypeStruct((t, SB_HEADS * SB_DH), BF16),
        compiler_params=_cparams(3),
        name="sb_attention",
    )(sb_q, sb_kv, sb_kv)


def _offdiag_segments(lo, hi):
    if hi - lo <= SUBCHUNK:
        return []
    mid = (lo + hi) // 2
    return [(lo, mid, hi)] + _offdiag_segments(lo, mid) + _offdiag_segments(mid, hi)


def _pad_rows(piece, lo, hi, total):
    parts = []
    if lo:
        parts.append(jnp.zeros((lo, piece.shape[1]), piece.dtype))
    parts.append(piece)
    if total - hi:
        parts.append(jnp.zeros((total - hi, piece.shape[1]), piece.dtype))
    return jnp.concatenate(parts, axis=0) if len(parts) > 1 else piece


def _hgrn_body(q_ref, f_ref, v_ref, g_ref, loglb_ref, log1mlb_ref, omlb_ref, onorm_ref,
               o_ref, cum_ref, k_ref, vrow_ref, state_ref, *, seq):
    c = CHUNK
    dk = HGRN_DK
    row = lax.broadcasted_iota(jnp.int32, (c, c), 0)
    col = lax.broadcasted_iota(jnp.int32, (c, c), 1)
    tril = (col <= row).astype(BF16)
    tril3 = jnp.concatenate([tril, tril, tril], axis=1)
    sub = lax.broadcasted_iota(jnp.int32, (SUBLANES, dk), 0)
    segs = _offdiag_segments(0, c)
    groups = c // SUBLANES
    groups_per_sub = SUBCHUNK // SUBLANES
    state_ref[...] = jnp.zeros_like(state_ref)

    def one_head(r0, hh, slot):
        sl = slice(hh * dk, (hh + 1) * dk)
        z = f_ref[pl.ds(r0, c), sl]
        log_sig = jnp.minimum(z, 0.0) - jnp.log(1.0 + jnp.exp(-jnp.abs(z)))
        y = log1mlb_ref[:, sl] + log_sig
        log_lb = loglb_ref[:, sl]
        log_f = jnp.maximum(log_lb, y) + jnp.log(1.0 + jnp.exp(-jnp.abs(log_lb - y)))
        kk = omlb_ref[:, sl] * jnp.exp(log_sig - z)
        hi = log_f.astype(BF16)
        rem = log_f - hi.astype(F32)
        mid = rem.astype(BF16)
        lo = (rem - mid.astype(F32)).astype(BF16)
        cum = jnp.dot(tril3, jnp.concatenate([hi, mid, lo], axis=0), preferred_element_type=F32) * LOG2E
        cum_ref[slot] = cum
        k_ref[slot] = kk
        qc = q_ref[pl.ds(r0, c), sl]
        vc = v_ref[pl.ds(r0, c), sl]
        vrow_ref[slot] = vc
        last = cum[c - 1:c, :]
        state_t = state_ref[hh]
        qe = (qc * jnp.exp2(cum)).astype(BF16)
        o_mm = lax.dot_general(qe, state_t.astype(BF16), _NT, preferred_element_type=F32)
        if segs:
            qa, ka = [], []
            for (lo_r, mid_r, hi_r) in segs:
                ref = cum[mid_r - 1:mid_r, :]
                qa.append(_pad_rows(qc[mid_r:hi_r] * jnp.exp2(cum[mid_r:hi_r] - ref), mid_r, hi_r, c))
                ka.append(_pad_rows(kk[lo_r:mid_r] * jnp.exp2(ref - cum[lo_r:mid_r]), lo_r, mid_r, c))
            s_off = lax.dot_general(jnp.concatenate(qa, axis=1).astype(BF16),
                                    jnp.concatenate(ka, axis=1).astype(BF16), _NT,
                                    preferred_element_type=F32)
            o_mm = o_mm + jnp.dot(s_off.astype(BF16), vc.astype(BF16), preferred_element_type=F32)
        o_parts = [o_mm[g * SUBLANES:(g + 1) * SUBLANES] for g in range(groups)]
        for j in range(c):
            kj = k_ref[slot, pl.ds(j, 1), :]
            cj = cum_ref[slot, pl.ds(j, 1), :]
            vj = vrow_ref[slot, pl.ds(j, 1), :]
            g0 = j // SUBLANES
            g_end = (j // SUBCHUNK + 1) * groups_per_sub
            for g in range(g0, g_end):
                qv = qc[g * SUBLANES:(g + 1) * SUBLANES]
                cv = cum[g * SUBLANES:(g + 1) * SUBLANES]
                w = qv * kj * jnp.exp2(cv - cj)
                if g == g0 and j % SUBLANES:
                    w = jnp.where(sub >= j % SUBLANES, w, 0.0)
                o_parts[g] = o_parts[g] + jnp.sum(w, axis=1, keepdims=True) * vj
        o = jnp.concatenate(o_parts, axis=0)
        kd = (kk * jnp.exp2(last - cum)).astype(BF16)
        state_ref[hh] = state_t * jnp.exp2(last) + jnp.dot(vc.T.astype(BF16), kd,
                                                          preferred_element_type=F32)
        o = o * lax.rsqrt(jnp.mean(o * o, axis=-1, keepdims=True) + EPS)
        hg = g_ref[pl.ds(r0, c), sl]
        o = o * onorm_ref[:, sl] * (hg * jax.nn.sigmoid(hg))
        o_ref[pl.ds(r0, c), sl] = o.astype(o_ref.dtype)

    def chunks(ci, _):
        for u in range(HGRN_UNROLL):
            r0 = pl.multiple_of((ci * HGRN_UNROLL + u) * c, c)
            for hh in range(2):
                one_head(r0, hh, 2 * u + hh)
        return 0

    assert seq % (c * HGRN_UNROLL) == 0
    lax.fori_loop(0, seq // (c * HGRN_UNROLL), chunks, 0)


def _hgrn(hg_in, log_lb, log_1mlb, om_lb, out_norm, *, batch, seq):
    t = hg_in.shape[0]
    pairs = HGRN_HEADS // 2
    w = 2 * LANES
    vec = pl.BlockSpec((1, w), lambda b, h: (0, h))
    blk = lambda off: pl.BlockSpec((seq, w), lambda b, h: (b, off * pairs + h))
    return pl.pallas_call(
        functools.partial(_hgrn_body, seq=seq),
        grid=(batch, pairs),
        in_specs=[blk(0), blk(1), blk(2), blk(3), vec, vec, vec, vec],
        out_specs=pl.BlockSpec((seq, w), lambda b, h: (b, h)),
        out_shape=jax.ShapeDtypeStruct((t, HGRN_HEADS * HGRN_DV), BF16),
        scratch_shapes=[pltpu.VMEM((2 * HGRN_UNROLL, CHUNK, HGRN_DK), F32),
                        pltpu.VMEM((2 * HGRN_UNROLL, CHUNK, HGRN_DK), F32),
                        pltpu.VMEM((2 * HGRN_UNROLL, CHUNK, HGRN_DV), F32),
                        pltpu.VMEM((2, HGRN_DV, HGRN_DK), F32)],
        compiler_params=_cparams(2),
        name="hgrn2",
    )(hg_in, hg_in, hg_in, hg_in, log_lb.reshape(1, -1), log_1mlb.reshape(1, -1),
      om_lb.reshape(1, -1), out_norm.reshape(1, -1))


def _merge_body(ya_ref, yb_ref, yc_ref, wa_ref, wb_ref, wc_ref, ga_ref, gb_ref, gc_ref, o_ref):
    acc = ga_ref[...].astype(F32) * jnp.dot(ya_ref[...], wa_ref[...], preferred_element_type=F32)
    acc = acc + gb_ref[...].astype(F32) * jnp.dot(yb_ref[...], wb_ref[...], preferred_element_type=F32)
    acc = acc + gc_ref[...].astype(F32) * jnp.dot(yc_ref[...], wc_ref[...], preferred_element_type=F32)
    o_ref[...] = acc.astype(o_ref.dtype)


def _merge(ya, yb, yc, wa, wb, wc, gates, layer):
    m = ya.shape[0]
    d = wa.shape[2]
    tm = _tile(m, ROW_TILE, SUBLANES)
    tn = _tile(d, COL_TILE // 2, LANES)
    nj = d // tn
    row_blk = lambda arr: pl.BlockSpec((tm, arr.shape[1]), lambda i, j: (i, 0))
    col_blk = lambda arr: pl.BlockSpec((None, arr.shape[1], tn), lambda i, j: (layer, 0, j))
    gate_blk = lambda g: pl.BlockSpec((tm, tn), lambda i, j: (i, g * nj + j))
    return pl.pallas_call(
        _merge_body,
        grid=(m // tm, nj),
        in_specs=[row_blk(ya), row_blk(yb), row_blk(yc), col_blk(wa), col_blk(wb), col_blk(wc),
                  gate_blk(0), gate_blk(1), gate_blk(2)],
        out_specs=pl.BlockSpec((tm, tn), lambda i, j: (i, j)),
        out_shape=jax.ShapeDtypeStruct((m, d), BF16),
        compiler_params=_cparams(2),
        name="branch_merge",
    )(ya, yb, yc, wa, wb, wc, gates, gates, gates)


def _mm_res_body(a_ref, w_ref, res_ref, o_ref):
    o_ref[...] = res_ref[...] + jnp.dot(a_ref[...], w_ref[...], preferred_element_type=F32)


def _mm_res(a, w, res, layer, *, name):
    m, k = a.shape
    n = w.shape[2]
    tm = _tile(m, ROW_TILE, SUBLANES)
    tn = _tile(n, max(LANES, 2 * ROW_TILE * COL_TILE // k), LANES)
    return pl.pallas_call(
        _mm_res_body,
        grid=(m // tm, n // tn),
        in_specs=[pl.BlockSpec((tm, k), lambda i, j: (i, 0)),
                  pl.BlockSpec((None, k, tn), lambda i, j: (layer, 0, j)),
                  pl.BlockSpec((tm, tn), lambda i, j: (i, j))],
        out_specs=pl.BlockSpec((tm, tn), lambda i, j: (i, j)),
        out_shape=jax.ShapeDtypeStruct((m, n), F32),
        compiler_params=_cparams(2),
        name=name,
    )(a, w, res)


def _ffn_up_body(a_ref, wg_ref, wu_ref, cg_ref, cu_ref, o_ref, stage_ref, *, tm, tn, per_seq):
    i = pl.program_id(1)
    pad = SUBLANES

    @pl.when(i % per_seq == 0)
    def _():
        stage_ref[0:pad, :] = jnp.zeros((pad, 2 * tn), F32)

    @pl.when(i % per_seq != 0)
    def _():
        stage_ref[0:pad, :] = stage_ref[tm:tm + pad, :]

    a = a_ref[...]
    stage_ref[pad:pad + tm, 0:tn] = jnp.dot(a, wg_ref[...], preferred_element_type=F32)
    stage_ref[pad:pad + tm, tn:2 * tn] = jnp.dot(a, wu_ref[...], preferred_element_type=F32)

    def conv(lo, hi, c_ref):
        cw = c_ref[...]
        out = cw[CONV_W - 1:CONV_W, :] * stage_ref[pad:pad + tm, lo:hi]
        for back in range(1, CONV_W):
            out = out + (cw[CONV_W - 1 - back:CONV_W - back, :]
                         * stage_ref[pad - back:pad - back + tm, lo:hi])
        return out

    gate = conv(0, tn, cg_ref)
    up = conv(tn, 2 * tn, cu_ref)
    o_ref[...] = (gate * jax.nn.sigmoid(gate) * up).astype(o_ref.dtype)


def _ffn_up(a, w_up, conv_w, layer, *, seq):
    m, k = a.shape
    d_ff = w_up.shape[2] // 2
    tm = _tile(seq, ROW_TILE, SUBLANES)
    tn = _tile(d_ff, COL_TILE // 2, LANES)
    nj = d_ff // tn
    return pl.pallas_call(
        functools.partial(_ffn_up_body, tm=tm, tn=tn, per_seq=seq // tm),
        grid=(nj, m // tm),
        in_specs=[pl.BlockSpec((tm, k), lambda j, i: (i, 0)),
                  pl.BlockSpec((None, k, tn), lambda j, i: (layer, 0, j)),
                  pl.BlockSpec((None, k, tn), lambda j, i: (layer, 0, nj + j)),
                  pl.BlockSpec((CONV_W, tn), lambda j, i: (0, j)),
                  pl.BlockSpec((CONV_W, tn), lambda j, i: (0, nj + j))],
        out_specs=pl.BlockSpec((tm, tn), lambda j, i: (i, j)),
        out_shape=jax.ShapeDtypeStruct((m, d_ff), BF16),
        scratch_shapes=[pltpu.VMEM((tm + SUBLANES, 2 * tn), F32)],
        compiler_params=_cparams(2),
        name="ffn_up_conv_gate",
    )(a, w_up, w_up, conv_w, conv_w)


def _cast_body(x_ref, o_ref):
    o_ref[...] = x_ref[...].astype(o_ref.dtype)


def _cast_bf16(w):
    nl, k, n = w.shape
    rk = _tile(k, max(2 * SUBLANES, CAST_BLOCK_BYTES // (4 * n)), 2 * SUBLANES)
    return pl.pallas_call(
        _cast_body,
        grid=(nl, k // rk),
        in_specs=[pl.BlockSpec((None, rk, n), lambda l, r: (l, r, 0))],
        out_specs=pl.BlockSpec((None, rk, n), lambda l, r: (l, r, 0)),
        out_shape=jax.ShapeDtypeStruct(w.shape, BF16),
        compiler_params=_cparams(2),
        name="weight_cast",
    )(w)


def _w_in_body(*refs, n_later, n_lead, pieces):
    src_refs, extra_ref, o_ref = refs[:pieces], refs[pieces], refs[pieces + 1]
    j = pl.program_id(1)
    rp = src_refs[0].shape[0]

    @pl.when(j < n_later + n_lead)
    def _():
        for u in range(pieces):
            o_ref[u * rp:(u + 1) * rp, :] = src_refs[u][...].astype(o_ref.dtype)

    @pl.when(j >= n_later + n_lead)
    def _():
        o_ref[...] = extra_ref[...]


def _w_in_relayout(w_in_t, extra, *, n_lead_rows, n_skip_rows):
    nl, n_in, d = w_in_t.shape
    later_src = n_lead_rows + n_skip_rows
    n_later_rows = n_in - later_src
    step = _tile(int(np.gcd(n_later_rows, n_lead_rows)), 512, 2 * SUBLANES)
    rp = int(np.gcd(step, later_src))
    assert rp % (2 * SUBLANES) == 0 and n_later_rows % step == 0 and n_lead_rows % step == 0
    pieces = step // rp
    n_later, n_lead = n_later_rows // step, n_lead_rows // step
    n_extra = -(-extra.shape[1] // step)
    extra = jnp.pad(extra, ((0, 0), (0, n_extra * step - extra.shape[1]), (0, 0)))

    def src_spec(u):
        def idx(l, j):
            blk = jnp.where(j < n_later, later_src // rp + j * pieces + u,
                            jnp.where(j < n_later + n_lead, (j - n_later) * pieces + u, 0))
            return (l, blk, 0)
        return pl.BlockSpec((None, rp, d), idx)

    ex_idx = lambda l, j: (l, jnp.where(j >= n_later + n_lead, j - n_later - n_lead, 0), 0)
    return pl.pallas_call(
        functools.partial(_w_in_body, n_later=n_later, n_lead=n_lead, pieces=pieces),
        grid=(nl, n_later + n_lead + n_extra),
        in_specs=[src_spec(u) for u in range(pieces)] + [pl.BlockSpec((None, step, d), ex_idx)],
        out_specs=pl.BlockSpec((None, step, d), lambda l, j: (l, j, 0)),
        out_shape=jax.ShapeDtypeStruct((nl, (n_later + n_lead + n_extra) * step, d), BF16),
        compiler_params=_cparams(2),
        name="w_in_relayout",
    )(*([w_in_t] * pieces), extra)


def _rot_half_cols(w):
    shape = w.shape
    w = w.reshape(shape[0], -1, 2, MLA_ROPE // 2)
    return jnp.stack([-w[:, :, 1], w[:, :, 0]], axis=2).reshape(shape)


def kernel(x, w_in, mla_q_norm, mla_kv_norm, mla_w_uq, mla_w_uk, mla_w_uv, hgrn_lb_logits,
           hgrn_out_norm, w_branch_mla, w_branch_hgrn, w_branch_sb, w_out, mix_norm, ffn_norm,
           ffn_w_up, ffn_conv, ffn_w_down, final_norm):
    batch, seq, d = x.shape
    depth = w_in.shape[0]
    t = batch * seq
    q_lora = mla_q_norm.shape[1]
    kv_lora = mla_kv_norm.shape[1]
    hk = HGRN_HEADS * HGRN_DK
    hv = HGRN_HEADS * HGRN_DV
    sbw = SB_HEADS * SB_DH
    qk_dim = MLA_NOPE + MLA_ROPE
    splits = (q_lora, kv_lora, MLA_ROPE, hk, hk, hv, hv, sbw, sbw, sbw, d, d, d)

    inv = 1.0 / (ROPE_THETA ** (jnp.arange(0, MLA_ROPE, 2, dtype=F32) / MLA_ROPE))
    ang = jnp.arange(seq, dtype=F32)[:, None] * inv[None, :]
    cos2 = jnp.tile(jnp.cos(ang), (1, 2 * LANES // MLA_ROPE))
    sin2 = jnp.tile(jnp.sin(ang), (1, 2 * LANES // MLA_ROPE))

    lb_all = jnp.cumsum(jax.nn.softmax(hgrn_lb_logits.astype(F32), axis=0), axis=0)
    lb_all = lb_all - lb_all[0:1]

    wb_mla, wb_hgrn, wb_sb = _cast_bf16(w_branch_mla), _cast_bf16(w_branch_hgrn), _cast_bf16(w_branch_sb)
    w_out_b, w_up_b, w_down_b = _cast_bf16(w_out), _cast_bf16(ffn_w_up), _cast_bf16(ffn_w_down)

    n_lat = q_lora + kv_lora
    w_in_t = jnp.swapaxes(w_in, 1, 2)
    w_kr = w_in[:, :, n_lat:n_lat + MLA_ROPE].reshape(depth * d, MLA_ROPE)
    w_kr_rot = _rot_half_cols(w_kr)
    kr4 = jnp.concatenate([w_kr, w_kr, w_kr_rot, w_kr_rot], axis=1).astype(BF16).reshape(depth, d, 2 * LANES)
    w_in_x = _w_in_relayout(w_in_t, jnp.swapaxes(kr4, 1, 2), n_lead_rows=n_lat, n_skip_rows=MLA_ROPE)
    n_hgrn, n_sb, n_gate = 2 * hk + 2 * hv, 3 * sbw, 3 * d
    n_mla = n_lat + 2 * LANES
    c_sb, c_gate, c_mla = n_hgrn, n_hgrn + n_sb, n_hgrn + n_sb + n_gate
    assert sum(splits) == w_in.shape[2]

    xf = x.reshape(t, d)
    for l in range(depth):
        uq = mla_w_uq[l].reshape(q_lora, MLA_HEADS, qk_dim)
        w_qn = uq[:, :, :MLA_NOPE].reshape(q_lora, MLA_HEADS * MLA_NOPE).astype(BF16)
        w_qpe_f = uq[:, :, MLA_NOPE:].reshape(q_lora, MLA_HEADS * MLA_ROPE)
        w_qpe = w_qpe_f.astype(BF16)
        w_qper = _rot_half_cols(w_qpe_f).astype(BF16)
        w_kv = jnp.concatenate([mla_w_uk[l], mla_w_uv[l]], axis=1).astype(BF16)
        lb = lb_all[l]

        hn = _rmsnorm(xf, mix_norm[l], BF16)
        in_proj = functools.partial(_mm, hn, w_in_x, k=d, layer=l, b_rows=True)
        gates = in_proj(b_col0=c_gate, n=n_gate, out_dtype=BF16, act="sigmoid", name="in_proj_gates")
        hg_in = in_proj(b_col0=0, n=n_hgrn, out_dtype=F32, name="in_proj_hgrn")
        sb_q = in_proj(b_col0=c_sb, n=sbw, out_dtype=BF16, scale=SB_DH ** -0.5, name="in_proj_sb_q")
        sb_kv = in_proj(b_col0=c_sb + sbw, n=2 * sbw, out_dtype=BF16, name="in_proj_sb_kv")
        mla_in = in_proj(b_col0=c_mla, n=n_mla, out_dtype=F32, name="in_proj_mla")

        scale = qk_dim ** -0.5 * LOG2E
        qn =_mm(mla_in, w_qn, k=q_lora, a_blk=0, n=w_qn.shape[1], out_dtype=BF16,
                 norm_w=mla_q_norm[l], scale=scale, name="mla_q_nope")
        qpe = _qpe(mla_in, mla_q_norm[l], w_qpe, w_qper, cos2, sin2, k=q_lora, seq=seq, scale=scale)
        kv = _mm(mla_in, w_kv, k=kv_lora, a_blk=q_lora // kv_lora, n=w_kv.shape[1], out_dtype=BF16,
                 norm_w=mla_kv_norm[l], name="mla_kv_up")
        y_a = _mla_attention(qn, qpe, kv, mla_in, cos2, sin2, batch=batch, seq=seq)

        y_b = _hgrn(hg_in, jnp.log(lb), jnp.log1p(-lb), 1.0 - lb, hgrn_out_norm[l], batch=batch, seq=seq)
        y_c = _sb_attention(sb_q, sb_kv, batch=batch, seq=seq)

        merged = _merge(y_a, y_b, y_c, wb_mla, wb_hgrn, wb_sb, gates, l)
        xf = _mm_res(merged, w_out_b, xf, l, name="out_proj_residual")

        hn2 = _rmsnorm(xf, ffn_norm[l], BF16)
        act = _ffn_up(hn2, w_up_b, ffn_conv[l], l, seq=seq)
        xf = _mm_res(act, w_down_b, xf, l, name="ffn_down_residual")

    out = _rmsnorm(xf, final_norm, x.dtype)
    return out.reshape(batch, seq, d)
```

```python
import functools

import numpy as np
import jax
import jax.numpy as jnp
from jax import lax
from jax.experimental import pallas as pl
from jax.experimental.pallas import tpu as pltpu

CHUNK = 64
EPS = 1e-6
MLA_HEADS = 16
MLA_NOPE = 128
MLA_ROPE = 64
MLA_V = 128
ROPE_THETA = 10000.0
HGRN_HEADS = 8
HGRN_DK = 128
HGRN_DV = 128
SB_HEADS = 8
SB_DH = 128
CONV_W = 3
SUBCHUNK = 16
HGRN_UNROLL = 8

LANES = 128
SUBLANES = 8
VMEM_LIMIT_BYTES = 56 * 1024 * 1024
CAST_BLOCK_BYTES = 8 * 1024 * 1024
ROW_TILE = 1024
COL_TILE = 1024
MLA_TILE = 512
SB_Q_TILE = 512
SB_GROUP = 4
SB_K_TILE = 256
NEG_BIG = -1e30
LOG2E = 1.4426950408889634
assert CHUNK & (CHUNK - 1) == 0

F32 = jnp.float32
BF16 = jnp.bfloat16
_NT = (((1,), (1,)), ((), ()))


def _tile(n, pref, align):
    t = (min(pref, n) // align) * align
    while t >= align:
        if n % t == 0:
            return t
        t -= align
    return n


def _cparams(ndims):
    return pltpu.CompilerParams(dimension_semantics=("arbitrary",) * ndims,
                                vmem_limit_bytes=VMEM_LIMIT_BYTES)


def _rms_rows(x, w):
    ms = jnp.mean(x * x, axis=-1, keepdims=True)
    return x * lax.rsqrt(ms + EPS) * w


def _rmsnorm_body(x_ref, w_ref, o_ref):
    o_ref[...] = _rms_rows(x_ref[...], w_ref[...]).astype(o_ref.dtype)


def _rmsnorm(x2d, w, out_dtype):
    t, d = x2d.shape
    tm = _tile(t, 512, SUBLANES)
    return pl.pallas_call(
        _rmsnorm_body,
        grid=(t // tm,),
        in_specs=[pl.BlockSpec((tm, d), lambda i: (i, 0)),
                  pl.BlockSpec((1, d), lambda i: (0, 0))],
        out_specs=pl.BlockSpec((tm, d), lambda i: (i, 0)),
        out_shape=jax.ShapeDtypeStruct((t, d), out_dtype),
        compiler_params=_cparams(1),
        name="rmsnorm",
    )(x2d, w.reshape(1, d))


def _mm_body(*refs, norm, act, scale, b_rows):
    if norm:
        a_ref, nw_ref, b_ref, o_ref = refs
        a = _rms_rows(a_ref[...], nw_ref[...]).astype(BF16)
    else:
        a_ref, b_ref, o_ref = refs
        a = a_ref[...]
    if b_rows:
        acc = lax.dot_general(a, b_ref[...], _NT, preferred_element_type=F32)
    else:
        acc = jnp.dot(a, b_ref[...], preferred_element_type=F32)
    if act == "sigmoid":
        acc = jax.nn.sigmoid(acc)
    if scale is not None:
        acc = acc * scale
    o_ref[...] = acc.astype(o_ref.dtype)


def _mm(a, b, *, k, a_blk=0, b_col0=0, n, out_dtype, norm_w=None, act=None, scale=None, layer=None,
        b_rows=False, name="mm"):
    m = a.shape[0]
    tm = _tile(m, ROW_TILE, SUBLANES)
    tn = _tile(int(np.gcd(n, b_col0)) if b_col0 else n, COL_TILE, LANES)
    assert n % tn == 0 and b_col0 % tn == 0 and b.shape[-1 if b_rows else -2] == k
    j0 = b_col0 // tn
    if b_rows:
        b_spec = pl.BlockSpec((None, tn, k), lambda i, j: (layer, j0 + j, 0))
    elif layer is None:
        b_spec = pl.BlockSpec((k, tn), lambda i, j: (0, j0 + j))
    else:
        b_spec = pl.BlockSpec((None, k, tn), lambda i, j: (layer, 0, j0 + j))
    in_specs = [pl.BlockSpec((tm, k), lambda i, j: (i, a_blk))]
    args = [a]
    if norm_w is not None:
        in_specs.append(pl.BlockSpec((1, k), lambda i, j: (0, 0)))
        args.append(norm_w.reshape(1, k))
    in_specs.append(b_spec)
    args.append(b)
    return pl.pallas_call(
        functools.partial(_mm_body, norm=norm_w is not None, act=act, scale=scale, b_rows=b_rows),
        grid=(m // tm, n // tn),
        in_specs=in_specs,
        out_specs=pl.BlockSpec((tm, tn), lambda i, j: (i, j)),
        out_shape=jax.ShapeDtypeStruct((m, n), out_dtype),
        compiler_params=_cparams(2),
        name=name,
    )(*args)


def _qpe_body(a_ref, nw_ref, w_ref, wr_ref, cos_ref, sin_ref, o_ref, *, scale, reps):
    a = _rms_rows(a_ref[...], nw_ref[...]).astype(BF16)
    p = jnp.dot(a, w_ref[...], preferred_element_type=F32)
    pr = jnp.dot(a, wr_ref[...], preferred_element_type=F32)
    cos = jnp.tile(cos_ref[...], (1, reps))
    sin = jnp.tile(sin_ref[...], (1, reps))
    o_ref[...] = ((p * cos + pr * sin) * scale).astype(o_ref.dtype)


def _qpe(mla_in, norm_w, w_pe, w_per, cos2, sin2, *, k, seq, scale):
    m = mla_in.shape[0]
    n = w_pe.shape[1]
    tm = _tile(seq, ROW_TILE, SUBLANES)
    tn = _tile(n, COL_TILE // 2, LANES)
    per_seq = seq // tm
    return pl.pallas_call(
        functools.partial(_qpe_body, scale=scale, reps=tn // LANES),
        grid=(m // tm, n // tn),
        in_specs=[pl.BlockSpec((tm, k), lambda i, j: (i, 0)),
                  pl.BlockSpec((1, k), lambda i, j: (0, 0)),
                  pl.BlockSpec((k, tn), lambda i, j: (0, j)),
                  pl.BlockSpec((k, tn), lambda i, j: (0, j)),
                  pl.BlockSpec((tm, LANES), lambda i, j: (i % per_seq, 0)),
                  pl.BlockSpec((tm, LANES), lambda i, j: (i % per_seq, 0))],
        out_specs=pl.BlockSpec((tm, tn), lambda i, j: (i, j)),
        out_shape=jax.ShapeDtypeStruct((m, n), BF16),
        compiler_params=_cparams(2),
        name="mla_q_rope",
    )(mla_in, norm_w.reshape(1, k), w_pe, w_per, cos2, sin2)


def _mla_body(qn_ref, qpe_ref, kn_ref, v_ref, kpe_ref, kper_ref, cos_ref, sin_ref,
              o_ref, kcat_ref, *, tq, tk):
    i = pl.program_id(2)

    @pl.when(i == 0)
    def _():
        kpe = (kpe_ref[...] * cos_ref[...] + kper_ref[...] * sin_ref[...]).astype(BF16)
        for hh in range(2):
            kcat_ref[hh, :, :LANES] = kn_ref[:, hh * LANES:(hh + 1) * LANES]
            kcat_ref[hh, :, LANES:] = kpe

    lane = lax.broadcasted_iota(jnp.int32, (tq, LANES), 1)
    qpe = qpe_ref[...]
    zero = jnp.zeros((tq, LANES), BF16)
    qs = [jnp.concatenate([qn_ref[:, 0:LANES], jnp.where(lane < MLA_ROPE, qpe, zero)], axis=1),
          jnp.concatenate([qn_ref[:, LANES:2 * LANES], jnp.where(lane >= MLA_ROPE, qpe, zero)], axis=1)]

    def block(r0, carry, visible):
        out = []
        for hh in range(2):
            m_prev, l_prev, acc = carry[hh]
            k = kcat_ref[hh, pl.ds(r0, tk), :]
            v = v_ref[pl.ds(r0, tk), hh * LANES:(hh + 1) * LANES]
            s = lax.dot_general(qs[hh], k, _NT, preferred_element_type=F32)
            if visible is not None:
                s = jnp.where(visible, s, NEG_BIG)
            m_new = jnp.maximum(m_prev, jnp.max(s, axis=1, keepdims=True))
            alpha = jnp.exp2(m_prev - m_new)
            p = jnp.exp2(s - m_new)
            l_new = alpha * l_prev + jnp.sum(p, axis=1, keepdims=True)
            acc = alpha * acc + jnp.dot(p.astype(BF16), v, preferred_element_type=F32)
            out.append((m_new, l_new, acc))
        return tuple(out)

    one = (jnp.full((tq, 1), NEG_BIG, F32), jnp.zeros((tq, 1), F32), jnp.zeros((tq, MLA_V), F32))
    carry = lax.fori_loop(0, i, lambda j, c: block(pl.multiple_of(j * tk, tk), c, None),
                          (one, one))
    row = lax.broadcasted_iota(jnp.int32, (tq, tk), 0)
    col = lax.broadcasted_iota(jnp.int32, (tq, tk), 1)
    visible = col <= jnp.bitwise_or(row, CHUNK - 1)
    carry = block(pl.multiple_of(i * tk, tk), carry, visible)
    for hh in range(2):
        _, l_fin, acc = carry[hh]
        o_ref[:, hh * LANES:(hh + 1) * LANES] = (acc / l_fin).astype(o_ref.dtype)


def _mla_attention(qn, qpe, kv, mla_in, cos2, sin2, *, batch, seq):
    t = qn.shape[0]
    tq = tk = _tile(seq, MLA_TILE, CHUNK)
    nq = seq // tq
    pairs = MLA_HEADS // 2
    kpe_blk = mla_in.shape[1] // LANES - 2
    return pl.pallas_call(
        functools.partial(_mla_body, tq=tq, tk=tk),
        grid=(batch, pairs, nq),
        in_specs=[pl.BlockSpec((tq, 2 * LANES), lambda b, h, i: (b * nq + i, h)),
                  pl.BlockSpec((tq, LANES), lambda b, h, i: (b * nq + i, h)),
                  pl.BlockSpec((seq, 2 * LANES), lambda b, h, i: (b, h)),
                  pl.BlockSpec((seq, 2 * LANES), lambda b, h, i: (b, pairs + h)),
                  pl.BlockSpec((seq, LANES), lambda b, h, i: (b, kpe_blk)),
                  pl.BlockSpec((seq, LANES), lambda b, h, i: (b, kpe_blk + 1)),
                  pl.BlockSpec((seq, LANES), lambda b, h, i: (0, 0)),
                  pl.BlockSpec((seq, LANES), lambda b, h, i: (0, 0))],
        out_specs=pl.BlockSpec((tq, 2 * LANES), lambda b, h, i: (b * nq + i, h)),
        out_shape=jax.ShapeDtypeStruct((t, MLA_HEADS * MLA_V), BF16),
        scratch_shapes=[pltpu.VMEM((2, seq, 2 * LANES), BF16)],
        compiler_params=_cparams(3),
        name="mla_attention",
    )(qn, qpe, kv, kv, mla_in, mla_in, cos2, sin2)


def _sb_body(q_ref, k_ref, v_ref, o_ref, *, tq, tk):
    i = pl.program_id(2)
    row = lax.broadcasted_iota(jnp.int32, (tq, tk), 0)
    col = lax.broadcasted_iota(jnp.int32, (tq, tk), 1)
    krow = lax.broadcasted_iota(jnp.int32, (tk, tk), 0)
    kcol = lax.broadcasted_iota(jnp.int32, (tk, tk), 1)
    later = (krow > kcol).astype(BF16)
    qs = [q_ref[:, hh * LANES:(hh + 1) * LANES] for hh in range(SB_GROUP)]
    per_q = tq // tk

    def block(j, carry, diag):
        r0 = pl.multiple_of(j * tk, tk)
        strict = None if diag is None else (col + diag * tk < row)
        out = []
        for hh in range(SB_GROUP):
            acc, tail_c = carry[hh]
            k = k_ref[pl.ds(r0, tk), hh * LANES:(hh + 1) * LANES]
            v = v_ref[pl.ds(r0, tk), hh * LANES:(hh + 1) * LANES]
            z = lax.dot_general(qs[hh], k, _NT, preferred_element_type=F32)
            sp = jnp.maximum(z, 0.0) + jnp.log(1.0 + jnp.exp(-jnp.abs(z)))
            log_sig = z - sp
            if strict is not None:
                sp = jnp.where(strict, sp, 0.0)
            tail = tail_c - jnp.dot(sp.astype(BF16), later, preferred_element_type=F32)
            a = jnp.exp(log_sig + tail)
            if strict is not None:
                a = jnp.where(strict, a, 0.0)
            acc = acc + jnp.dot(a.astype(BF16), v, preferred_element_type=F32)
            tail_c = tail_c - jnp.sum(sp, axis=1, keepdims=True)
            out.append((acc, tail_c))
        return tuple(out)

    one = (jnp.zeros((tq, SB_DH), F32), jnp.zeros((tq, 1), F32))
    carry = (one,) * SB_GROUP
    for d in reversed(range(per_q)):
        carry = block(i * per_q + d, carry, d)
    n_before = i * per_q
    carry = lax.fori_loop(0, n_before, lambda jj, c: block(n_before - 1 - jj, c, None), carry)
    for hh in range(SB_GROUP):
        o_ref[:, hh * LANES:(hh + 1) * LANES] = carry[hh][0].astype(o_ref.dtype)


def _sb_attention(sb_q, sb_kv, *, batch, seq):
    t = sb_q.shape[0]
    tq = _tile(seq, SB_Q_TILE, LANES)
    tk = _tile(tq, SB_K_TILE, LANES)
    nq = seq // tq
    groups = SB_HEADS // SB_GROUP
    w = SB_GROUP * LANES
    return pl.pallas_call(
        functools.partial(_sb_body, tq=tq, tk=tk),
        grid=(batch, groups, nq),
        in_specs=[pl.BlockSpec((tq, w), lambda b, h, i: (b * nq + i, h)),
                  pl.BlockSpec((seq, w), lambda b, h, i: (b, h)),
                  pl.BlockSpec((seq, w), lambda b, h, i: (b, groups + h))],
        out_specs=pl.BlockSpec((tq, w), lambda b, h, i: (b * nq + i, h)),
        out_shape=jax.ShapeDtypeStruct((t, SB_HEADS * SB_DH), BF16),
        compiler_params=_cparams(3),
        name="sb_attention",
    )(sb_q, sb_kv, sb_kv)


def _offdiag_segments(lo, hi):
    if hi - lo <= SUBCHUNK:
        return []
    mid = (lo + hi) // 2
    return [(lo, mid, hi)] + _offdiag_segments(lo, mid) + _offdiag_segments(mid, hi)


def _pad_rows(piece, lo, hi, total):
    parts = []
    if lo:
        parts.append(jnp.zeros((lo, piece.shape[1]), piece.dtype))
    parts.append(piece)
    if total - hi:
        parts.append(jnp.zeros((total - hi, piece.shape[1]), piece.dtype))
    return jnp.concatenate(parts, axis=0) if len(parts) > 1 else piece


def _hgrn_body(q_ref, f_ref, v_ref, g_ref, loglb_ref, log1mlb_ref, omlb_ref, onorm_ref,
               o_ref, cum_ref, k_ref, vrow_ref, state_ref, *, seq):
    c = CHUNK
    dk = HGRN_DK
    row = lax.broadcasted_iota(jnp.int32, (c, c), 0)
    col = lax.broadcasted_iota(jnp.int32, (c, c), 1)
    tril = (col <= row).astype(BF16)
    tril3 = jnp.concatenate([tril, tril, tril], axis=1)
    sub = lax.broadcasted_iota(jnp.int32, (SUBLANES, dk), 0)
    segs = _offdiag_segments(0, c)
    groups = c // SUBLANES
    groups_per_sub = SUBCHUNK // SUBLANES
    state_ref[...] = jnp.zeros_like(state_ref)

    def one_head(r0, hh, slot):
        sl = slice(hh * dk, (hh + 1) * dk)
        z = f_ref[pl.ds(r0, c), sl]
        log_sig = jnp.minimum(z, 0.0) - jnp.log(1.0 + jnp.exp(-jnp.abs(z)))
        y = log1mlb_ref[:, sl] + log_sig
        log_lb = loglb_ref[:, sl]
        log_f = jnp.maximum(log_lb, y) + jnp.log(1.0 + jnp.exp(-jnp.abs(log_lb - y)))
        kk = omlb_ref[:, sl] * jnp.exp(log_sig - z)
        hi = log_f.astype(BF16)
        rem = log_f - hi.astype(F32)
        mid = rem.astype(BF16)
        lo = (rem - mid.astype(F32)).astype(BF16)
        cum = jnp.dot(tril3, jnp.concatenate([hi, mid, lo], axis=0), preferred_element_type=F32) * LOG2E
        cum_ref[slot] = cum
        k_ref[slot] = kk
        qc = q_ref[pl.ds(r0, c), sl]
        vc = v_ref[pl.ds(r0, c), sl]
        vrow_ref[slot] = vc
        last = cum[c - 1:c, :]
        state_t = state_ref[hh]
        qe = (qc * jnp.exp2(cum)).astype(BF16)
        o_mm = lax.dot_general(qe, state_t.astype(BF16), _NT, preferred_element_type=F32)
        if segs:
            qa, ka = [], []
            for (lo_r, mid_r, hi_r) in segs:
                ref = cum[mid_r - 1:mid_r, :]
                qa.append(_pad_rows(qc[mid_r:hi_r] * jnp.exp2(cum[mid_r:hi_r] - ref), mid_r, hi_r, c))
                ka.append(_pad_rows(kk[lo_r:mid_r] * jnp.exp2(ref - cum[lo_r:mid_r]), lo_r, mid_r, c))
            s_off = lax.dot_general(jnp.concatenate(qa, axis=1).astype(BF16),
                                    jnp.concatenate(ka, axis=1).astype(BF16), _NT,
                                    preferred_element_type=F32)
            o_mm = o_mm + jnp.dot(s_off.astype(BF16), vc.astype(BF16), preferred_element_type=F32)
        o_parts = [o_mm[g * SUBLANES:(g + 1) * SUBLANES] for g in range(groups)]
        for j in range(c):
            kj = k_ref[slot, pl.ds(j, 1), :]
            cj = cum_ref[slot, pl.ds(j, 1), :]
            vj = vrow_ref[slot, pl.ds(j, 1), :]
            g0 = j // SUBLANES
            g_end = (j // SUBCHUNK + 1) * groups_per_sub
            for g in range(g0, g_end):
                qv = qc[g * SUBLANES:(g + 1) * SUBLANES]
                cv = cum[g * SUBLANES:(g + 1) * SUBLANES]
                w = qv * kj * jnp.exp2(cv - cj)
                if g == g0 and j % SUBLANES:
                    w = jnp.where(sub >= j % SUBLANES, w, 0.0)
                o_parts[g] = o_parts[g] + jnp.sum(w, axis=1, keepdims=True) * vj
        o = jnp.concatenate(o_parts, axis=0)
        kd = (kk * jnp.exp2(last - cum)).astype(BF16)
        state_ref[hh] = state_t * jnp.exp2(last) + jnp.dot(vc.T.astype(BF16), kd,
                                                          preferred_element_type=F32)
        o = o * lax.rsqrt(jnp.mean(o * o, axis=-1, keepdims=True) + EPS)
        hg = g_ref[pl.ds(r0, c), sl]
        o = o * onorm_ref[:, sl] * (hg * jax.nn.sigmoid(hg))
        o_ref[pl.ds(r0, c), sl] = o.astype(o_ref.dtype)

    def chunks(ci, _):
        for u in range(HGRN_UNROLL):
            r0 = pl.multiple_of((ci * HGRN_UNROLL + u) * c, c)
            for hh in range(2):
                one_head(r0, hh, 2 * u + hh)
        return 0

    assert seq % (c * HGRN_UNROLL) == 0
    lax.fori_loop(0, seq // (c * HGRN_UNROLL), chunks, 0)


def _hgrn(hg_in, log_lb, log_1mlb, om_lb, out_norm, *, batch, seq):
    t = hg_in.shape[0]
    pairs = HGRN_HEADS // 2
    w = 2 * LANES
    vec = pl.BlockSpec((1, w), lambda b, h: (0, h))
    blk = lambda off: pl.BlockSpec((seq, w), lambda b, h: (b, off * pairs + h))
    return pl.pallas_call(
        functools.partial(_hgrn_body, seq=seq),
        grid=(batch, pairs),
        in_specs=[blk(0), blk(1), blk(2), blk(3), vec, vec, vec, vec],
        out_specs=pl.BlockSpec((seq, w), lambda b, h: (b, h)),
        out_shape=jax.ShapeDtypeStruct((t, HGRN_HEADS * HGRN_DV), BF16),
        scratch_shapes=[pltpu.VMEM((2 * HGRN_UNROLL, CHUNK, HGRN_DK), F32),
                        pltpu.VMEM((2 * HGRN_UNROLL, CHUNK, HGRN_DK), F32),
                        pltpu.VMEM((2 * HGRN_UNROLL, CHUNK, HGRN_DV), F32),
                        pltpu.VMEM((2, HGRN_DV, HGRN_DK), F32)],
        compiler_params=_cparams(2),
        name="hgrn2",
    )(hg_in, hg_in, hg_in, hg_in, log_lb.reshape(1, -1), log_1mlb.reshape(1, -1),
      om_lb.reshape(1, -1), out_norm.reshape(1, -1))


def _merge_body(ya_ref, yb_ref, yc_ref, wa_ref, wb_ref, wc_ref, ga_ref, gb_ref, gc_ref, o_ref):
    acc = ga_ref[...].astype(F32) * jnp.dot(ya_ref[...], wa_ref[...], preferred_element_type=F32)
    acc = acc + gb_ref[...].astype(F32) * jnp.dot(yb_ref[...], wb_ref[...], preferred_element_type=F32)
    acc = acc + gc_ref[...].astype(F32) * jnp.dot(yc_ref[...], wc_ref[...], preferred_element_type=F32)
    o_ref[...] = acc.astype(o_ref.dtype)


def _merge(ya, yb, yc, wa, wb, wc, gates, layer):
    m = ya.shape[0]
    d = wa.shape[2]
    tm = _tile(m, ROW_TILE, SUBLANES)
    tn = _tile(d, COL_TILE // 2, LANES)
    nj = d // tn
    row_blk = lambda arr: pl.BlockSpec((tm, arr.shape[1]), lambda i, j: (i, 0))
    col_blk = lambda arr: pl.BlockSpec((None, arr.shape[1], tn), lambda i, j: (layer, 0, j))
    gate_blk = lambda g: pl.BlockSpec((tm, tn), lambda i, j: (i, g * nj + j))
    return pl.pallas_call(
        _merge_body,
        grid=(m // tm, nj),
        in_specs=[row_blk(ya), row_blk(yb), row_blk(yc), col_blk(wa), col_blk(wb), col_blk(wc),
                  gate_blk(0), gate_blk(1), gate_blk(2)],
        out_specs=pl.BlockSpec((tm, tn), lambda i, j: (i, j)),
        out_shape=jax.ShapeDtypeStruct((m, d), BF16),
        compiler_params=_cparams(2),
        name="branch_merge",
    )(ya, yb, yc, wa, wb, wc, gates, gates, gates)


def _mm_res_body(a_ref, w_ref, res_ref, o_ref):
    o_ref[...] = res_ref[...] + jnp.dot(a_ref[...], w_ref[...], preferred_element_type=F32)


def _mm_res(a, w, res, layer, *, name):
    m, k = a.shape
    n = w.shape[2]
    tm = _tile(m, ROW_TILE, SUBLANES)
    tn = _tile(n, max(LANES, 2 * ROW_TILE * COL_TILE // k), LANES)
    return pl.pallas_call(
        _mm_res_body,
        grid=(m // tm, n // tn),
        in_specs=[pl.BlockSpec((tm, k), lambda i, j: (i, 0)),
                  pl.BlockSpec((None, k, tn), lambda i, j: (layer, 0, j)),
                  pl.BlockSpec((tm, tn), lambda i, j: (i, j))],
        out_specs=pl.BlockSpec((tm, tn), lambda i, j: (i, j)),
        out_shape=jax.ShapeDtypeStruct((m, n), F32),
        compiler_params=_cparams(2),
        name=name,
    )(a, w, res)


def _ffn_up_body(a_ref, wg_ref, wu_ref, cg_ref, cu_ref, o_ref, stage_ref, *, tm, tn, per_seq):
    i = pl.program_id(1)
    pad = SUBLANES

    @pl.when(i % per_seq == 0)
    def _():
        stage_ref[0:pad, :] = jnp.zeros((pad, 2 * tn), F32)

    @pl.when(i % per_seq != 0)
    def _():
        stage_ref[0:pad, :] = stage_ref[tm:tm + pad, :]

    a = a_ref[...]
    stage_ref[pad:pad + tm, 0:tn] = jnp.dot(a, wg_ref[...], preferred_element_type=F32)
    stage_ref[pad:pad + tm, tn:2 * tn] = jnp.dot(a, wu_ref[...], preferred_element_type=F32)

    def conv(lo, hi, c_ref):
        cw = c_ref[...]
        out = cw[CONV_W - 1:CONV_W, :] * stage_ref[pad:pad + tm, lo:hi]
        for back in range(1, CONV_W):
            out = out + (cw[CONV_W - 1 - back:CONV_W - back, :]
                         * stage_ref[pad - back:pad - back + tm, lo:hi])
        return out

    gate = conv(0, tn, cg_ref)
    up = conv(tn, 2 * tn, cu_ref)
    o_ref[...] = (gate * jax.nn.sigmoid(gate) * up).astype(o_ref.dtype)


def _ffn_up(a, w_up, conv_w, layer, *, seq):
    m, k = a.shape
    d_ff = w_up.shape[2] // 2
    tm = _tile(seq, ROW_TILE, SUBLANES)
    tn = _tile(d_ff, COL_TILE // 2, LANES)
    nj = d_ff // tn
    return pl.pallas_call(
        functools.partial(_ffn_up_body, tm=tm, tn=tn, per_seq=seq // tm),
        grid=(nj, m // tm),
        in_specs=[pl.BlockSpec((tm, k), lambda j, i: (i, 0)),
                  pl.BlockSpec((None, k, tn), lambda j, i: (layer, 0, j)),
                  pl.BlockSpec((None, k, tn), lambda j, i: (layer, 0, nj + j)),
                  pl.BlockSpec((CONV_W, tn), lambda j, i: (0, j)),
                  pl.BlockSpec((CONV_W, tn), lambda j, i: (0, nj + j))],
        out_specs=pl.BlockSpec((tm, tn), lambda j, i: (i, j)),
        out_shape=jax.ShapeDtypeStruct((m, d_ff), BF16),
        scratch_shapes=[pltpu.VMEM((tm + SUBLANES, 2 * tn), F32)],
        compiler_params=_cparams(2),
        name="ffn_up_conv_gate",
    )(a, w_up, w_up, conv_w, conv_w)


def _cast_body(x_ref, o_ref):
    o_ref[...] = x_ref[...].astype(o_ref.dtype)


def _cast_bf16(w):
    nl, k, n = w.shape
    rk = _tile(k, max(2 * SUBLANES, CAST_BLOCK_BYTES // (4 * n)), 2 * SUBLANES)
    return pl.pallas_call(
        _cast_body,
        grid=(nl, k // rk),
        in_specs=[pl.BlockSpec((None, rk, n), lambda l, r: (l, r, 0))],
        out_specs=pl.BlockSpec((None, rk, n), lambda l, r: (l, r, 0)),
        out_shape=jax.ShapeDtypeStruct(w.shape, BF16),
        compiler_params=_cparams(2),
        name="weight_cast",
    )(w)


def _w_in_body(*refs, n_later, n_lead, pieces):
    src_refs, extra_ref, o_ref = refs[:pieces], refs[pieces], refs[pieces + 1]
    j = pl.program_id(1)
    rp = src_refs[0].shape[0]

    @pl.when(j < n_later + n_lead)
    def _():
        for u in range(pieces):
            o_ref[u * rp:(u + 1) * rp, :] = src_refs[u][...].astype(o_ref.dtype)

    @pl.when(j >= n_later + n_lead)
    def _():
        o_ref[...] = extra_ref[...]


def _w_in_relayout(w_in_t, extra, *, n_lead_rows, n_skip_rows):
    nl, n_in, d = w_in_t.shape
    later_src = n_lead_rows + n_skip_rows
    n_later_rows = n_in - later_src
    step = _tile(int(np.gcd(n_later_rows, n_lead_rows)), 512, 2 * SUBLANES)
    rp = int(np.gcd(step, later_src))
    assert rp % (2 * SUBLANES) == 0 and n_later_rows % step == 0 and n_lead_rows % step == 0
    pieces = step // rp
    n_later, n_lead = n_later_rows // step, n_lead_rows // step
    n_extra = -(-extra.shape[1] // step)
    extra = jnp.pad(extra, ((0, 0), (0, n_extra * step - extra.shape[1]), (0, 0)))

    def src_spec(u):
        def idx(l, j):
            blk = jnp.where(j < n_later, later_src // rp + j * pieces + u,
                            jnp.where(j < n_later + n_lead, (j - n_later) * pieces + u, 0))
            return (l, blk, 0)
        return pl.BlockSpec((None, rp, d), idx)

    ex_idx = lambda l, j: (l, jnp.where(j >= n_later + n_lead, j - n_later - n_lead, 0), 0)
    return pl.pallas_call(
        functools.partial(_w_in_body, n_later=n_later, n_lead=n_lead, pieces=pieces),
        grid=(nl, n_later + n_lead + n_extra),
        in_specs=[src_spec(u) for u in range(pieces)] + [pl.BlockSpec((None, step, d), ex_idx)],
        out_specs=pl.BlockSpec((None, step, d), lambda l, j: (l, j, 0)),
        out_shape=jax.ShapeDtypeStruct((nl, (n_later + n_lead + n_extra) * step, d), BF16),
        compiler_params=_cparams(2),
        name="w_in_relayout",
    )(*([w_in_t] * pieces), extra)


def _rot_half_cols(w):
    shape = w.shape
    w = w.reshape(shape[0], -1, 2, MLA_ROPE // 2)
    return jnp.stack([-w[:, :, 1], w[:, :, 0]], axis=2).reshape(shape)


def kernel(x, w_in, mla_q_norm, mla_kv_norm, mla_w_uq, mla_w_uk, mla_w_uv, hgrn_lb_logits,
           hgrn_out_norm, w_branch_mla, w_branch_hgrn, w_branch_sb, w_out, mix_norm, ffn_norm,
           ffn_w_up, ffn_conv, ffn_w_down, final_norm):
    batch, seq, d = x.shape
    depth = w_in.shape[0]
    t = batch * seq
    q_lora = mla_q_norm.shape[1]
    kv_lora = mla_kv_norm.shape[1]
    hk = HGRN_HEADS * HGRN_DK
    hv = HGRN_HEADS * HGRN_DV
    sbw = SB_HEADS * SB_DH
    qk_dim = MLA_NOPE + MLA_ROPE
    splits = (q_lora, kv_lora, MLA_ROPE, hk, hk, hv, hv, sbw, sbw, sbw, d, d, d)

    inv = 1.0 / (ROPE_THETA ** (jnp.arange(0, MLA_ROPE, 2, dtype=F32) / MLA_ROPE))
    ang = jnp.arange(seq, dtype=F32)[:, None] * inv[None, :]
    cos2 = jnp.tile(jnp.cos(ang), (1, 2 * LANES // MLA_ROPE))
    sin2 = jnp.tile(jnp.sin(ang), (1, 2 * LANES // MLA_ROPE))

    lb_all = jnp.cumsum(jax.nn.softmax(hgrn_lb_logits.astype(F32), axis=0), axis=0)
    lb_all = lb_all - lb_all[0:1]

    wb_mla, wb_hgrn, wb_sb = _cast_bf16(w_branch_mla), _cast_bf16(w_branch_hgrn), _cast_bf16(w_branch_sb)
    w_out_b, w_up_b, w_down_b = _cast_bf16(w_out), _cast_bf16(ffn_w_up), _cast_bf16(ffn_w_down)

    n_lat = q_lora + kv_lora
    w_in_t = jnp.swapaxes(w_in, 1, 2)
    w_kr = w_in[:, :, n_lat:n_lat + MLA_ROPE].reshape(depth * d, MLA_ROPE)
    w_kr_rot = _rot_half_cols(w_kr)
    kr4 = jnp.concatenate([w_kr, w_kr, w_kr_rot, w_kr_rot], axis=1).astype(BF16).reshape(depth, d, 2 * LANES)
    w_in_x = _w_in_relayout(w_in_t, jnp.swapaxes(kr4, 1, 2), n_lead_rows=n_lat, n_skip_rows=MLA_ROPE)
    n_hgrn, n_sb, n_gate = 2 * hk + 2 * hv, 3 * sbw, 3 * d
    n_mla = n_lat + 2 * LANES
    c_sb, c_gate, c_mla = n_hgrn, n_hgrn + n_sb, n_hgrn + n_sb + n_gate
    assert sum(splits) == w_in.shape[2]

    xf = x.reshape(t, d)
    for l in range(depth):
        uq = mla_w_uq[l].reshape(q_lora, MLA_HEADS, qk_dim)
        w_qn = uq[:, :, :MLA_NOPE].reshape(q_lora, MLA_HEADS * MLA_NOPE).astype(BF16)
        w_qpe_f = uq[:, :, MLA_NOPE:].reshape(q_lora, MLA_HEADS * MLA_ROPE)
        w_qpe = w_qpe_f.astype(BF16)
        w_qper = _rot_half_cols(w_qpe_f).astype(BF16)
        w_kv = jnp.concatenate([mla_w_uk[l], mla_w_uv[l]], axis=1).astype(BF16)
        lb = lb_all[l]

        hn = _rmsnorm(xf, mix_norm[l], BF16)
        in_proj = functools.partial(_mm, hn, w_in_x, k=d, layer=l, b_rows=True)
        gates = in_proj(b_col0=c_gate, n=n_gate, out_dtype=BF16, act="sigmoid", name="in_proj_gates")
        hg_in = in_proj(b_col0=0, n=n_hgrn, out_dtype=F32, name="in_proj_hgrn")
        sb_q = in_proj(b_col0=c_sb, n=sbw, out_dtype=BF16, scale=SB_DH ** -0.5, name="in_proj_sb_q")
        sb_kv = in_proj(b_col0=c_sb + sbw, n=2 * sbw, out_dtype=BF16, name="in_proj_sb_kv")
        mla_in = in_proj(b_col0=c_mla, n=n_mla, out_dtype=F32, name="in_proj_mla")

        scale = qk_dim ** -0.5 * LOG2E
        qn =_mm(mla_in, w_qn, k=q_lora, a_blk=0, n=w_qn.shape[1], out_dtype=BF16,
                 norm_w=mla_q_norm[l], scale=scale, name="mla_q_nope")
        qpe = _qpe(mla_in, mla_q_norm[l], w_qpe, w_qper, cos2, sin2, k=q_lora, seq=seq, scale=scale)
        kv = _mm(mla_in, w_kv, k=kv_lora, a_blk=q_lora // kv_lora, n=w_kv.shape[1], out_dtype=BF16,
                 norm_w=mla_kv_norm[l], name="mla_kv_up")
        y_a = _mla_attention(qn, qpe, kv, mla_in, cos2, sin2, batch=batch, seq=seq)

        y_b = _hgrn(hg_in, jnp.log(lb), jnp.log1p(-lb), 1.0 - lb, hgrn_out_norm[l], batch=batch, seq=seq)
        y_c = _sb_attention(sb_q, sb_kv, batch=batch, seq=seq)

        merged = _merge(y_a, y_b, y_c, wb_mla, wb_hgrn, wb_sb, gates, l)
        xf = _mm_res(merged, w_out_b, xf, l, name="out_proj_residual")

        hn2 = _rmsnorm(xf, ffn_norm[l], BF16)
        act = _ffn_up(hn2, w_up_b, ffn_conv[l], l, seq=seq)
        xf = _mm_res(act, w_down_b, xf, l, name="ffn_down_residual")

    out = _rmsnorm(xf, final_norm, x.dtype)
    return out.reshape(batch, seq, d)
```
